```python
import functools
import jax, jax.numpy as jnp
from jax import lax
import numpy as np

D_MODEL = 2048
BATCH = 8
SEQ = 2048
DEPTH = 2
DEC_BATCH = 128
DEC_SEQ = 1
PAST_LEN = 16384
PAGE_SIZE = 128

N_HEADS = 8
HEAD_DIM = 64
MIX_W = N_HEADS * HEAD_DIM
N_BRANCH = 4
D_FF = 2 * D_MODEL
PLE_DIM = 256
Q_BLOCK = 128
ROPE_THETA = 500000.0
ROT_DIM = HEAD_DIM // 4
NSA_KV = 1
NSA_REP = N_HEADS // NSA_KV
NSA_CMP = 32
NSA_SEL = 64
NSA_TOPK = 16
NSA_WINDOW = 512
GM_CHUNK = 128
GM_GROUPS = N_HEADS
GM_CH = HEAD_DIM
MLA_DQ = D_MODEL // 4
MLA_DC = D_MODEL // 16
MLA_DR = HEAD_DIM // 2
MLA_THETA = 10000.0
FOX_KV = 1
FOX_REP = N_HEADS // FOX_KV
FOX_BIAS = 2.0
EPS = 1e-6
NEG = -1e30
FORCE = 1e4
ATT_SCALE = HEAD_DIM ** -0.5
MLA_SCALE = (HEAD_DIM + MLA_DR) ** -0.5

IN_SPLITS = (
    ('nsa_q', MIX_W), ('nsa_kv', 6 * NSA_KV * HEAD_DIM), ('nsa_g', 3 * N_HEADS),
    ('gm_u', MIX_W), ('gm_v', MIX_W),
    ('mla_cq', MLA_DQ), ('mla_ckv', MLA_DC), ('mla_kr', MLA_DR),
    ('fox_q', MIX_W), ('fox_kv', 2 * FOX_KV * HEAD_DIM), ('fox_f', N_HEADS),
    ('branch_g', N_BRANCH * D_MODEL),
)
D_IN = (MIX_W + 6 * NSA_KV * HEAD_DIM + 3 * N_HEADS + 2 * MIX_W + MLA_DQ + MLA_DC + MLA_DR
        + MIX_W + 2 * FOX_KV * HEAD_DIM + N_HEADS + N_BRANCH * D_MODEL)

kernel_name = 'hybrid_nsa_gmlp_mla_fox_decode_step'


def rmsnorm(x, g):
    xf = x.astype(jnp.float32)
    y = xf * lax.rsqrt(jnp.mean(xf * xf, axis=-1, keepdims=True) + EPS)
    return (y * g.astype(jnp.float32)).astype(x.dtype)


def layernorm(x, g):
    xf = x.astype(jnp.float32)
    xc = xf - jnp.mean(xf, axis=-1, keepdims=True)
    y = xc * lax.rsqrt(jnp.mean(xc * xc, axis=-1, keepdims=True) + EPS)
    return (y * g.astype(jnp.float32)).astype(x.dtype)


def rope(x, pos, rot, theta):
    half = rot // 2
    inv = theta ** (-jnp.arange(half, dtype=jnp.float32) / half)
    ang = pos.astype(jnp.float32)[:, None] * inv[None, :]
    cos = jnp.cos(ang)[:, None, :]
    sin = jnp.sin(ang)[:, None, :]
    xf = x[..., :rot].astype(jnp.float32)
    x1, x2 = xf[..., :half], xf[..., half:]
    xr = jnp.concatenate([x1 * cos - x2 * sin, x2 * cos + x1 * sin], axis=-1).astype(x.dtype)
    return jnp.concatenate([xr, x[..., rot:]], axis=-1)


def split_proj(z):
    out, off = {}, 0
    for name, n in IN_SPLITS:
        out[name] = z[..., off:off + n]
        off += n
    return out


def swiglu(x, w_in, w_out):
    gu = x @ w_in
    return (jax.nn.silu(gu[..., :D_FF]) * gu[..., D_FF:]) @ w_out


def unblock(o):
    nb, b, qb, w = o.shape
    return o.transpose(1, 0, 2, 3).reshape(b, nb * qb, w)


def rotate_k(kv, pos):
    return jnp.stack([rope(kv[:, :, 0], pos, ROT_DIM, ROPE_THETA), kv[:, :, 1]], axis=2)


def nsa_project(pc, pos):
    B, T = pc['nsa_q'].shape[:2]
    q = pc['nsa_q'].reshape(B, T, N_HEADS, HEAD_DIM)
    q_rot = rope(q, pos, ROT_DIM, ROPE_THETA).reshape(B, T, NSA_KV, NSA_REP, HEAD_DIM)
    q = q.reshape(B, T, NSA_KV, NSA_REP, HEAD_DIM)
    kv = pc['nsa_kv'].reshape(B, T, 3, 2, NSA_KV, HEAD_DIM)
    gates = jax.nn.sigmoid(pc['nsa_g'].astype(jnp.float32)).astype(q.dtype).reshape(B, T, NSA_KV, NSA_REP, 3)
    return q, q_rot, kv[:, :, 0], rotate_k(kv[:, :, 1], pos), rotate_k(kv[:, :, 2], pos), gates


def nsa_compress(kv, w):
    B, L = kv.shape[:2]
    blocks = kv.reshape((B, L // NSA_CMP, NSA_CMP) + kv.shape[2:]).astype(jnp.float32)
    return jnp.mean(blocks * w[:, :, None, :].astype(jnp.float32), axis=2).astype(kv.dtype)


def sel_block_view(kv):
    B, L = kv.shape[:2]
    return kv.reshape(B, L // NSA_SEL, NSA_SEL, 2, NSA_KV, HEAD_DIM).transpose(0, 4, 1, 2, 3, 5)


def nsa_attend(q, q_rot, gates, cmp_blocks, gather_sel, kv_win, win_pos, q_pos):
    B, T = q.shape[:2]
    nc = cmp_blocks.shape[1]
    ns = nc * NSA_CMP // NSA_SEL
    qp = q_pos[None, :, None, None, None]
    s = jnp.einsum('btgrd,bngd->btgrn', q, cmp_blocks[:, :, 0], preferred_element_type=jnp.float32) * ATT_SCALE
    ok = (jnp.arange(nc) * NSA_CMP + NSA_CMP - 1) <= qp
    p = jax.nn.softmax(jnp.where(ok, s, NEG), axis=-1) * ok
    o_cmp = jnp.einsum('btgrn,bngd->btgrd', p.astype(q.dtype), cmp_blocks[:, :, 1])
    imp = p.sum(axis=3).reshape(B, T, NSA_KV, ns, NSA_SEL // NSA_CMP).sum(-1)
    blk = jnp.arange(ns)
    cur = (q_pos // NSA_SEL)[None, :, None, None]
    imp = jnp.where((blk == 0) | (blk == cur) | (blk == cur - 1), FORCE, jnp.where(blk > cur, NEG, imp))
    _, idx = lax.top_k(imp, min(NSA_TOPK, ns))
    kvs = gather_sel(idx)
    s = jnp.einsum('btgrd,btgkjd->btgrkj', q_rot, kvs[..., 0, :], preferred_element_type=jnp.float32) * ATT_SCALE
    tok = idx[..., None] * NSA_SEL + jnp.arange(NSA_SEL)
    ok = (tok <= q_pos[None, :, None, None, None])[:, :, :, None]
    sh = s.shape
    p = jax.nn.softmax(jnp.where(ok, s, NEG).reshape(sh[:4] + (-1,)), axis=-1).reshape(sh)
    o_sel = jnp.einsum('btgrkj,btgkjd->btgrd', p.astype(q.dtype), kvs[..., 1, :])
    s = jnp.einsum('btgrd,blgd->btgrl', q_rot, kv_win[:, :, 0], preferred_element_type=jnp.float32) * ATT_SCALE
    dist = q_pos[:, None] - win_pos[None, :]
    ok = ((dist >= 0) & (dist <= NSA_WINDOW) & (win_pos >= 0)[None, :])[None, :, None, None, :]
    p = jax.nn.softmax(jnp.where(ok, s, NEG), axis=-1)
    o_win = jnp.einsum('btgrl,blgd->btgrd', p.astype(q.dtype), kv_win[:, :, 1])
    o = gates[..., 0:1] * o_cmp + gates[..., 1:2] * o_sel + gates[..., 2:3] * o_win
    return o.reshape(B, T, MIX_W)


def nsa_prompt(pc, cmp_w):
    B, S = pc['nsa_q'].shape[:2]
    pos = jnp.arange(S)
    q, q_rot, kv_cmp, kv_sel, kv_win, gates = nsa_project(pc, pos)
    cmp_blocks = nsa_compress(kv_cmp, cmp_w)
    sel_b = sel_block_view(kv_sel)
    win_pad = jnp.pad(kv_win, ((0, 0), (NSA_WINDOW, 0), (0, 0), (0, 0), (0, 0)))
    bi = jnp.arange(B)[:, None, None, None]
    gi = jnp.arange(NSA_KV)[None, None, :, None]

    def gather(idx):
        return sel_b[bi, gi, idx]

    def block(i):
        q0 = i * Q_BLOCK
        sl = lambda a: lax.dynamic_slice_in_dim(a, q0, Q_BLOCK, axis=1)
        kvw = lax.dynamic_slice_in_dim(win_pad, q0, NSA_WINDOW + Q_BLOCK, axis=1)
        wpos = q0 - NSA_WINDOW + jnp.arange(NSA_WINDOW + Q_BLOCK)
        return nsa_attend(sl(q), sl(q_rot), sl(gates), cmp_blocks, gather, kvw, wpos, q0 + jnp.arange(Q_BLOCK))

    o = unblock(lax.map(block, jnp.arange(S // Q_BLOCK)))
    return o, kv_cmp, kv_sel, kv_win[:, S - min(NSA_WINDOW, S):]


def nsa_sample(pc, cmp_w, l, cache_cmp, cache_sel, win_buf, page_table):
    B, T = pc['nsa_q'].shape[:2]
    pos = PAST_LEN + jnp.arange(T)
    q, q_rot, kv_cmp, kv_sel, kv_win, gates = nsa_project(pc, pos)
    tp = -(-T // NSA_SEL) * NSA_SEL

    def pad_rows(a):
        return jnp.pad(a, ((0, 0), (0, tp - T)) + ((0, 0),) * (a.ndim - 2))

    past_cmp = cache_cmp[l, page_table].reshape(B, PAST_LEN, 2, NSA_KV, HEAD_DIM)
    cmp_blocks = jnp.concatenate([nsa_compress(past_cmp, cmp_w), nsa_compress(pad_rows(kv_cmp), cmp_w)], axis=1)
    n_past = PAST_LEN // NSA_SEL
    tail = sel_block_view(pad_rows(kv_sel))
    bi = jnp.arange(B)[:, None, None, None]
    gi = jnp.arange(NSA_KV)[None, None, :, None]
    kvi = jnp.arange(2)

    def gather(idx):
        start = jnp.minimum(idx, n_past - 1) * NSA_SEL
        phys = page_table[bi, start // PAGE_SIZE]
        rows = (start % PAGE_SIZE)[..., None] + jnp.arange(NSA_SEL)
        past = cache_sel[l, phys[..., None, None], rows[..., None], kvi, gi[..., None, None]]
        new = tail[bi, gi, jnp.clip(idx - n_past, 0, tail.shape[2] - 1)]
        return jnp.where((idx < n_past)[..., None, None, None], past, new)

    wb = win_buf.shape[1]
    kvw = jnp.concatenate([win_buf, kv_win], axis=1)
    wpos = PAST_LEN - wb + jnp.arange(wb + T)
    o = nsa_attend(q, q_rot, gates, cmp_blocks, gather, kvw, wpos, pos)
    return o, kv_cmp, kv_sel, kvw[:, T:]


def gmlp_project(pc, gm_norm):
    B, T = pc['gm_u'].shape[:2]
    u = jax.nn.gelu(pc['gm_u'], approximate=False)
    v = layernorm(jax.nn.gelu(pc['gm_v'], approximate=False), gm_norm)
    return u, v.reshape(B, T, GM_GROUPS, GM_CH)


def gmlp_mix(u, v_chunks, ws, bs):
    c = ws.shape[-1]
    w = ws * jnp.tril(jnp.ones((c, c), ws.dtype))
    z = jnp.einsum('gij,bcjgd->bcigd', w, v_chunks) + bs.T[None, None, :, :, None]
    return u * z.reshape(u.shape)


def mla_project(pc, lw, pos):
    B, T = pc['mla_cq'].shape[:2]
    cq = rmsnorm(pc['mla_cq'], lw['mla_q_norm'])
    qf = (cq @ lw['mla_wq']).reshape(B, T, N_HEADS, HEAD_DIM + MLA_DR)
    q_lat = jnp.einsum('bthd,chd->bthc', qf[..., :HEAD_DIM], lw['mla_wuk'])
    q_rope = rope(qf[..., HEAD_DIM:], pos, MLA_DR, MLA_THETA)
    c = rmsnorm(pc['mla_ckv'], lw['mla_kv_norm'])
    kr = rope(pc['mla_kr'][:, :, None, :], pos, MLA_DR, MLA_THETA)[:, :, 0]
    return jnp.concatenate([q_lat, q_rope], axis=-1), jnp.concatenate([c, kr], axis=-1)


def mla_attend(qc, lat, q_pos, k_pos, wuv):
    B, T = qc.shape[:2]
    s = jnp.einsum('bthc,blc->bhtl', qc, lat, preferred_element_type=jnp.float32) * MLA_SCALE
    p = jax.nn.softmax(jnp.where(k_pos[None, :] <= q_pos[:, None], s, NEG), axis=-1)
    o_lat = jnp.einsum('bhtl,blc->bthc', p.astype(lat.dtype), lat)[..., :MLA_DC]
    return jnp.einsum('bthc,chd->bthd', o_lat, wuv).reshape(B, T, MIX_W)


def mla_prompt(pc, lw):
    S = pc['mla_cq'].shape[1]
    pos = jnp.arange(S)
    qc, lat = mla_project(pc, lw, pos)

    def block(i):
        q0 = i * Q_BLOCK
        return mla_attend(lax.dynamic_slice_in_dim(qc, q0, Q_BLOCK, axis=1), lat,
                          q0 + jnp.arange(Q_BLOCK), pos, lw['mla_wuv'])

    return unblock(lax.map(block, jnp.arange(S // Q_BLOCK))), lat


def fox_project(pc, fox_bf):
    B, T = pc['fox_q'].shape[:2]
    q = pc['fox_q'].reshape(B, T, FOX_KV, FOX_REP, HEAD_DIM)
    kv = pc['fox_kv'].reshape(B, T, 2, FOX_KV, HEAD_DIM)
    logf = jax.nn.log_sigmoid(pc['fox_f'].astype(jnp.float32) + fox_bf.astype(jnp.float32))
    return q, kv, logf


def fox_cum(logf):
    return logf - lax.cumsum(logf, axis=1, reverse=True)


def fox_attend(q, kv, fq, fk, q_pos, k_pos):
    B, T, G, R, _ = q.shape
    L = kv.shape[1]
    s = jnp.einsum('btgrd,blgd->bgrtl', q, kv[:, :, 0], preferred_element_type=jnp.float32) * ATT_SCALE
    fq = fq.reshape(B, T, G, R).transpose(0, 2, 3, 1)
    fk = fk.reshape(B, L, G, R).transpose(0, 2, 3, 1)
    s = s + fq[..., :, None] - fk[..., None, :]
    p = jax.nn.softmax(jnp.where(k_pos[None, :] <= q_pos[:, None], s, NEG), axis=-1)
    return jnp.einsum('bgrtl,blgd->btgrd', p.astype(q.dtype), kv[:, :, 1]).reshape(B, T, MIX_W)


def fox_prompt(pc, fox_bf):
    q, kv, logf = fox_project(pc, fox_bf)
    S = q.shape[1]
    F = fox_cum(logf)
    pos = jnp.arange(S)

    def block(i):
        q0 = i * Q_BLOCK
        sl = lambda a: lax.dynamic_slice_in_dim(a, q0, Q_BLOCK, axis=1)
        return fox_attend(sl(q), kv, sl(F), F, q0 + jnp.arange(Q_BLOCK), pos)

    return unblock(lax.map(block, jnp.arange(S // Q_BLOCK))), kv, logf.astype(kv.dtype)


def prompt_mixers(pc, lw):
    B, S = pc['gm_u'].shape[:2]
    o_nsa, nsa_cmp, nsa_sel, nsa_win = nsa_prompt(pc, lw['nsa_cmp_w'])
    u, v = gmlp_project(pc, lw['gm_norm'])
    o_gm = gmlp_mix(u, v.reshape(B, S // GM_CHUNK, GM_CHUNK, GM_GROUPS, GM_CH), lw['gm_ws'], lw['gm_bs'])
    o_mla, mla_rows = mla_prompt(pc, lw)
    o_fox, fox_kv, fox_logf = fox_prompt(pc, lw['fox_bf'])
    outs = jnp.stack([o_nsa, o_gm, o_mla, o_fox], axis=2)
    return outs, (nsa_cmp, nsa_sel, nsa_win, mla_rows, fox_kv, fox_logf)


def sample_mixers(pc, lw, l, cache_nsa_cmp, cache_nsa_sel, win_buf, cache_mla, cache_fox_kv, cache_fox_logf, page_table):
    B, T = pc['gm_u'].shape[:2]
    pos = PAST_LEN + jnp.arange(T)
    k_pos = jnp.arange(PAST_LEN + T)
    o_nsa, nsa_cmp, nsa_sel, nsa_win = nsa_sample(pc, lw['nsa_cmp_w'], l, cache_nsa_cmp, cache_nsa_sel, win_buf, page_table)
    u, v = gmlp_project(pc, lw['gm_norm'])
    o_gm = gmlp_mix(u, v[:, None], lw['gm_ws'][:, :T, :T], lw['gm_bs'][:, :T])
    qc, lat = mla_project(pc, lw, pos)
    lat_all = jnp.concatenate([cache_mla[l, page_table].reshape(B, PAST_LEN, MLA_DC + MLA_DR), lat], axis=1)
    o_mla = mla_attend(qc, lat_all, pos, k_pos, lw['mla_wuv'])
    q, kv, logf = fox_project(pc, lw['fox_bf'])
    kv_all = jnp.concatenate([cache_fox_kv[l, page_table].reshape(B, PAST_LEN, 2, FOX_KV, HEAD_DIM), kv], axis=1)
    logf_all = jnp.concatenate([cache_fox_logf[l, page_table].reshape(B, PAST_LEN, N_HEADS).astype(jnp.float32), logf], axis=1)
    F = fox_cum(logf_all)
    o_fox = fox_attend(q, kv_all, F[:, PAST_LEN:], F, pos, k_pos)
    outs = jnp.stack([o_nsa, o_gm, o_mla, o_fox], axis=2)
    return outs, (nsa_cmp, nsa_sel, nsa_win, lat, kv, logf.astype(kv.dtype), v)


def layer_step(x, p_l, lw, mixers):
    B, T = x.shape[:2]
    x = x + 0.5 * swiglu(rmsnorm(x, lw['n_ffn1']), lw['ffn1_in'], lw['ffn1_out'])
    pc = split_proj(rmsnorm(x, lw['n_mix']) @ lw['w_in'])
    outs, new_state = mixers(pc)
    gates = jax.nn.sigmoid(pc['branch_g'].astype(jnp.float32)).astype(x.dtype).reshape(B, T, N_BRANCH, D_MODEL)
    merged = jnp.sum(gates * jnp.einsum('btnw,nwd->btnd', outs, lw['w_br']), axis=2)
    x = x + merged @ lw['w_o']
    x = x + 0.5 * swiglu(rmsnorm(x, lw['n_ffn2']), lw['ffn2_in'], lw['ffn2_out'])
    x = x + jax.nn.sigmoid(rmsnorm(x, lw['n_ple']) @ lw['ple_wg']) * (p_l @ lw['ple_wp'])
    return x, new_state


def setup_inputs(seed: int = 0) -> dict:
    key = jax.random.key(seed)
    keys = iter(jax.random.split(key, 48))

    def nrm(shape, scale=1.0):
        return scale * jax.random.normal(next(keys), shape, jnp.float32)

    def gain(shape):
        return 1.0 + 0.01 * jax.random.normal(next(keys), shape, jnp.float32)

    n_pages = PAST_LEN // PAGE_SIZE
    n_pool = (DEC_BATCH * n_pages * 5) // 4
    wb = min(NSA_WINDOW, PAST_LEN)
    return {
        'x_prompt': nrm((BATCH, SEQ, D_MODEL)),
        'x_sample': nrm((DEC_BATCH, DEC_SEQ, D_MODEL)),
        'cache_nsa_cmp': nrm((DEPTH, n_pool, PAGE_SIZE, 2, NSA_KV, HEAD_DIM)),
        'cache_nsa_sel': nrm((DEPTH, n_pool, PAGE_SIZE, 2, NSA_KV, HEAD_DIM)),
        'state_nsa_win': nrm((DEPTH, DEC_BATCH, wb, 2, NSA_KV, HEAD_DIM)),
        'cache_mla': nrm((DEPTH, n_pool, PAGE_SIZE, MLA_DC + MLA_DR)),
        'cache_fox_kv': nrm((DEPTH, n_pool, PAGE_SIZE, 2, FOX_KV, HEAD_DIM)),
        'cache_fox_logf': jax.nn.log_sigmoid(FOX_BIAS + nrm((DEPTH, n_pool, PAGE_SIZE, N_HEADS))),
        'page_table': jax.random.permutation(next(keys), n_pool)[:DEC_BATCH * n_pages].reshape(DEC_BATCH, n_pages).astype(jnp.int32),
        'p_prompt': nrm((DEPTH, BATCH, SEQ, PLE_DIM)),
        'p_sample': nrm((DEPTH, DEC_BATCH, DEC_SEQ, PLE_DIM)),
        'n_ffn1': gain((DEPTH, D_MODEL)),
        'ffn1_in': nrm((DEPTH, D_MODEL, 2 * D_FF), D_MODEL ** -0.5),
        'ffn1_out': nrm((DEPTH, D_FF, D_MODEL), D_FF ** -0.5),
        'n_mix': gain((DEPTH, D_MODEL)),
        'w_in': nrm((DEPTH, D_MODEL, D_IN), D_MODEL ** -0.5),
        'nsa_cmp_w': 1.0 + nrm((DEPTH, NSA_CMP, 2, HEAD_DIM), 0.1),
        'gm_norm': gain((DEPTH, MIX_W)),
        'gm_ws': nrm((DEPTH, GM_GROUPS, GM_CHUNK, GM_CHUNK), GM_CHUNK ** -0.5),
        'gm_bs': 1.0 + nrm((DEPTH, GM_GROUPS, GM_CHUNK), 0.1),
        'mla_q_norm': gain((DEPTH, MLA_DQ)),
        'mla_wq': nrm((DEPTH, MLA_DQ, N_HEADS * (HEAD_DIM + MLA_DR)), MLA_DQ ** -0.5),
        'mla_kv_norm': gain((DEPTH, MLA_DC)),
        'mla_wuk': nrm((DEPTH, MLA_DC, N_HEADS, HEAD_DIM), MLA_DC ** -0.5),
        'mla_wuv': nrm((DEPTH, MLA_DC, N_HEADS, HEAD_DIM), MLA_DC ** -0.5),
        'fox_bf': FOX_BIAS + nrm((DEPTH, N_HEADS), 0.1),
        'w_br': nrm((DEPTH, N_BRANCH, MIX_W, D_MODEL), MIX_W ** -0.5),
        'w_o': nrm((DEPTH, D_MODEL, D_MODEL), D_MODEL ** -0.5),
        'n_ffn2': gain((DEPTH, D_MODEL)),
        'ffn2_in': nrm((DEPTH, D_MODEL, 2 * D_FF), D_MODEL ** -0.5),
        'ffn2_out': nrm((DEPTH, D_FF, D_MODEL), D_FF ** -0.5),
        'n_ple': gain((DEPTH, D_MODEL)),
        'ple_wg': nrm((DEPTH, D_MODEL, D_MODEL), D_MODEL ** -0.5),
        'ple_wp': nrm((DEPTH, PLE_DIM, D_MODEL), PLE_DIM ** -0.5),
        'norm_final': gain((D_MODEL,)),
    }


def reference(x_prompt, x_sample, cache_nsa_cmp, cache_nsa_sel, state_nsa_win, cache_mla, cache_fox_kv,
              cache_fox_logf, page_table, p_prompt, p_sample, n_ffn1, ffn1_in, ffn1_out, n_mix, w_in, nsa_cmp_w,
              gm_norm, gm_ws, gm_bs, mla_q_norm, mla_wq, mla_kv_norm, mla_wuk, mla_wuv, fox_bf, w_br, w_o,
              n_ffn2, ffn2_in, ffn2_out, n_ple, ple_wg, ple_wp, norm_final):
    xp, xs = x_prompt, x_sample
    st_p, st_s = [], []
    for l in range(DEPTH):
        lw = dict(n_ffn1=n_ffn1[l], ffn1_in=ffn1_in[l], ffn1_out=ffn1_out[l], n_mix=n_mix[l], w_in=w_in[l],
                  nsa_cmp_w=nsa_cmp_w[l], gm_norm=gm_norm[l], gm_ws=gm_ws[l], gm_bs=gm_bs[l],
                  mla_q_norm=mla_q_norm[l], mla_wq=mla_wq[l], mla_kv_norm=mla_kv_norm[l], mla_wuk=mla_wuk[l],
                  mla_wuv=mla_wuv[l], fox_bf=fox_bf[l], w_br=w_br[l], w_o=w_o[l], n_ffn2=n_ffn2[l],
                  ffn2_in=ffn2_in[l], ffn2_out=ffn2_out[l], n_ple=n_ple[l], ple_wg=ple_wg[l], ple_wp=ple_wp[l])
        xp, sp = layer_step(xp, p_prompt[l], lw, functools.partial(prompt_mixers, lw=lw))
        xs, ss = layer_step(xs, p_sample[l], lw, functools.partial(
            sample_mixers, lw=lw, l=l, cache_nsa_cmp=cache_nsa_cmp, cache_nsa_sel=cache_nsa_sel,
            win_buf=state_nsa_win[l], cache_mla=cache_mla, cache_fox_kv=cache_fox_kv,
            cache_fox_logf=cache_fox_logf, page_table=page_table))
        st_p.append(sp)
        st_s.append(ss)
    nsa_cmp_p, nsa_sel_p, nsa_win_p, mla_p, fox_kv_p, fox_logf_p = [jnp.stack(a) for a in zip(*st_p)]
    nsa_cmp_s, nsa_sel_s, nsa_win_s, mla_s, fox_kv_s, fox_logf_s, gm_v_s = [jnp.stack(a) for a in zip(*st_s)]
    y_prompt = rmsnorm(xp, norm_final)
    y_sample = rmsnorm(xs, norm_final)
    return (y_prompt, y_sample, nsa_cmp_p, nsa_sel_p, nsa_win_p, mla_p, fox_kv_p, fox_logf_p,
            nsa_cmp_s, nsa_sel_s, nsa_win_s, mla_s, fox_kv_s, fox_logf_s, gm_v_s)
```

```python
import functools

import numpy as np
import jax
import jax.numpy as jnp
from jax import lax
from jax.experimental import pallas as pl
from jax.experimental.pallas import tpu as pltpu

D_MODEL = 2048
N_HEADS = 8
HEAD_DIM = 64
MIX_W = N_HEADS * HEAD_DIM
N_BRANCH = 4
D_FF = 2 * D_MODEL
PLE_DIM = 256
ROPE_THETA = 500000.0
ROT_DIM = HEAD_DIM // 4
NSA_CMP = 32
NSA_SEL = 64
NSA_TOPK = 16
NSA_WINDOW = 512
GM_CHUNK = 128
MLA_DQ = D_MODEL // 4
MLA_DC = D_MODEL // 16
MLA_DR = HEAD_DIM // 2
MLA_THETA = 10000.0
EPS = 1e-6
NEG = -1e30
FORCE = 1e4
ATT_SCALE = HEAD_DIM ** -0.5
MLA_SCALE = (HEAD_DIM + MLA_DR) ** -0.5
PAGE_SIZE = 128

F32 = jnp.float32
BF16 = jnp.bfloat16
LANES = 128
MLA_QW = 2 * LANES
VMEM_LIMIT = 56 * 1024 * 1024

Z_NSA_Q, Z_NSA_QS, Z_GM_U, Z_GM_V, Z_MLA_CQ, Z_FOX_Q = (i * MIX_W for i in range(6))
Z_NSA_CMP = 6 * MIX_W
Z_NSA_SEL = Z_NSA_CMP + LANES
Z_NSA_WIN = Z_NSA_SEL + LANES
Z_NSA_KSS = Z_NSA_WIN + LANES
Z_NSA_KSW = Z_NSA_KSS + LANES
Z_NSA_G = Z_NSA_KSW + LANES
Z_MLA_CKV = Z_NSA_G + LANES
Z_MLA_KR = Z_MLA_CKV + LANES
Z_MLA_KRS = Z_MLA_KR + LANES
Z_FOX_KV = Z_MLA_KRS + LANES
Z_FOX_F = Z_FOX_KV + LANES
Z_W = Z_FOX_F + LANES


def _cparams(sem):
    return pltpu.CompilerParams(dimension_semantics=sem, vmem_limit_bytes=VMEM_LIMIT)


def _const_spec(shape):
    nd = len(shape)
    return pl.BlockSpec(shape, lambda *_: (0,) * nd, pipeline_mode=pl.Buffered(1))


def _rms(x, g):
    return x * lax.rsqrt(jnp.mean(x * x, axis=-1, keepdims=True) + EPS) * g


def _dot(a, b):
    return jnp.dot(a, b, preferred_element_type=F32)


def _dot_nt(a, b):
    return lax.dot_general(a, b, (((1,), (1,)), ((), ())), preferred_element_type=F32)


def _ffn_kernel(x_ref, g_ref, wg_ref, wu_ref, wo_ref, o_ref, h_scr, acc_scr):
    k = pl.program_id(1)

    @pl.when(k == 0)
    def _():
        h_scr[...] = _rms(x_ref[...], g_ref[...]).astype(BF16)
        acc_scr[...] = jnp.zeros_like(acc_scr)

    h = h_scr[...]
    g = _dot(h, wg_ref[...])
    u = _dot(h, wu_ref[...])
    a = (g * jax.nn.sigmoid(g)) * u
    acc_scr[...] += _dot(a.astype(BF16), wo_ref[...])

    @pl.when(k == pl.num_programs(1) - 1)
    def _():
        o_ref[...] = x_ref[...] + 0.5 * acc_scr[...]


def _ffn(x, g, w_in, w_out, tm, tc):
    n, d = x.shape
    dff = w_out.shape[0]
    nc = dff // tc
    return pl.pallas_call(
        _ffn_kernel,
        out_shape=jax.ShapeDtypeStruct((n, d), F32),
        grid=(n // tm, nc),
        in_specs=[
            pl.BlockSpec((tm, d), lambda i, k: (i, 0)),
            pl.BlockSpec((1, d), lambda i, k: (0, 0)),
            pl.BlockSpec((d, tc), lambda i, k: (0, k)),
            pl.BlockSpec((d, tc), lambda i, k: (0, k + nc)),
            pl.BlockSpec((tc, d), lambda i, k: (k, 0)),
        ],
        out_specs=pl.BlockSpec((tm, d), lambda i, k: (i, 0)),
        scratch_shapes=[pltpu.VMEM((tm, d), BF16), pltpu.VMEM((tm, d), F32)],
        compiler_params=_cparams(("parallel", "arbitrary")),
        name="ffn",
    )(x, g.reshape(1, d), w_in, w_in, w_out)


def _proj_kernel(x_ref, g_ref, w_ref, o_ref):
    h = _rms(x_ref[...], g_ref[...]).astype(BF16)
    o_ref[...] = _dot(h, w_ref[...])


def _proj(x, g, w, tm):
    n, d = x.shape
    zw = w.shape[1]
    return pl.pallas_call(
        _proj_kernel,
        out_shape=jax.ShapeDtypeStruct((n, zw), F32),
        grid=(n // tm,),
        in_specs=[
            pl.BlockSpec((tm, d), lambda i: (i, 0)),
            pl.BlockSpec((1, d), lambda i: (0, 0)),
            _const_spec((d, zw)),
        ],
        out_specs=pl.BlockSpec((tm, zw), lambda i: (i, 0)),
        compiler_params=_cparams(("parallel",)),
        name="in_proj",
    )(x, g.reshape(1, d), w)


def _merge_kernel(x_ref, g_ref, o0_ref, o1_ref, o2_ref, o3_ref, wg_ref, wbr_ref, wo_ref,
                  out_ref, h_scr, acc_scr):
    s = pl.program_id(1)
    half = s % 2

    @pl.when(s == 0)
    def _():
        h_scr[...] = _rms(x_ref[...], g_ref[...]).astype(BF16)
        acc_scr[...] = jnp.zeros_like(acc_scr)

    gate = jax.nn.sigmoid(_dot(h_scr[...], wg_ref[...]))
    for b, o_ref in enumerate((o0_ref, o1_ref, o2_ref, o3_ref)):
        @pl.when(s // 2 == b)
        def _(o_ref=o_ref):
            acc_scr[half] += gate * _dot(o_ref[...], wbr_ref[0])

    @pl.when(s == pl.num_programs(1) - 1)
    def _():
        hd = wo_ref.shape[0] // 2
        y = _dot(acc_scr[0].astype(BF16), wo_ref[:hd, :])
        y += _dot(acc_scr[1].astype(BF16), wo_ref[hd:, :])
        out_ref[...] = x_ref[...] + y


def _merge(x, g, outs, w_gate, w_br, w_o, tm):
    n, d = x.shape
    hd = d // 2
    o_spec = pl.BlockSpec((tm, MIX_W), lambda i, s: (i, 0))
    return pl.pallas_call(
        _merge_kernel,
        out_shape=jax.ShapeDtypeStruct((n, d), F32),
        grid=(n // tm, 2 * N_BRANCH),
        in_specs=[
            pl.BlockSpec((tm, d), lambda i, s: (i, 0)),
            pl.BlockSpec((1, d), lambda i, s: (0, 0)),
            o_spec, o_spec, o_spec, o_spec,
            pl.BlockSpec((d, hd), lambda i, s: (0, s)),
            pl.BlockSpec((1, MIX_W, hd), lambda i, s: (s // 2, 0, s % 2)),
            _const_spec((d, d)),
        ],
        out_specs=pl.BlockSpec((tm, d), lambda i, s: (i, 0)),
        scratch_shapes=[pltpu.VMEM((tm, d), BF16), pltpu.VMEM((2, tm, hd), F32)],
        compiler_params=_cparams(("parallel", "arbitrary")),
        name="merge",
    )(x, g.reshape(1, d), *outs, w_gate, w_br, w_o)


def _ple_kernel(x_ref, g_ref, p_ref, wg_ref, wp_ref, gf_ref, o_ref, *, final):
    x = x_ref[...]
    gate = jax.nn.sigmoid(_dot(_rms(x, g_ref[...]).astype(BF16), wg_ref[...]))
    y = x + gate * _dot(p_ref[...].astype(BF16), wp_ref[...])
    if final:
        y = _rms(y, gf_ref[...])
    o_ref[...] = y


def _ple(x, g, p, w_g, w_p, g_final, final, tm):
    n, d = x.shape
    pd = p.shape[1]
    return pl.pallas_call(
        functools.partial(_ple_kernel, final=final),
        out_shape=jax.ShapeDtypeStruct((n, d), F32),
        grid=(n // tm,),
        in_specs=[
            pl.BlockSpec((tm, d), lambda i: (i, 0)),
            pl.BlockSpec((1, d), lambda i: (0, 0)),
            pl.BlockSpec((tm, pd), lambda i: (i, 0)),
            _const_spec((d, d)),
            _const_spec((pd, d)),
            pl.BlockSpec((1, d), lambda i: (0, 0)),
        ],
        out_specs=pl.BlockSpec((tm, d), lambda i: (i, 0)),
        compiler_params=_cparams(("parallel",)),
        name="ple",
    )(x, g.reshape(1, d), p, w_g, w_p, g_final.reshape(1, d))


def _gelu(x):
    return 0.5 * x * (1.0 + lax.erf(x * np.float32(np.sqrt(0.5))))


def _log_sigmoid(x):
    return jnp.minimum(x, 0.0) - jnp.log1p(jnp.exp(-jnp.abs(x)))


def _prep_kernel(zq_ref, zqs_ref, zu_ref, zv_ref, zcq_ref, zsel_ref, zwin_ref, zkss_ref, zksw_ref,
                 zg_ref, zckv_ref, zkr_ref, zkrs_ref, zf_ref, tn_ref, tm_ref,
                 gmn_ref, qn_ref, kvn_ref, bf_ref, wqn_ref, wbd_ref, wa_ref, wb_ref,
                 oq_ref, oqr_ref, osel_ref, owin_ref, ogate_ref, ou_ref, ov_ref, oqc_ref,
                 olat_ref, ologf_ref):
    tn = tn_ref[...]
    cos_q = jnp.concatenate([tn[:, 0:LANES]] * (MIX_W // LANES), axis=1)
    sin_q = jnp.concatenate([tn[:, LANES:2 * LANES]] * (MIX_W // LANES), axis=1)
    cos_k = tn[:, 2 * LANES:3 * LANES]
    sin_k = tn[:, 3 * LANES:4 * LANES]
    q = zq_ref[...]
    oq_ref[...] = (q * ATT_SCALE).astype(BF16)
    oqr_ref[...] = ((q * cos_q + zqs_ref[...] * sin_q) * ATT_SCALE).astype(BF16)
    osel_ref[...] = zsel_ref[...] * cos_k + zkss_ref[...] * sin_k
    owin_ref[...] = zwin_ref[...] * cos_k + zksw_ref[...] * sin_k
    ogate_ref[...] = jax.nn.sigmoid(zg_ref[...])

    ou_ref[...] = _gelu(zu_ref[...])
    gv = _gelu(zv_ref[...])
    gc = gv - jnp.mean(gv, axis=-1, keepdims=True)
    ov_ref[...] = gc * lax.rsqrt(jnp.mean(gc * gc, axis=-1, keepdims=True) + EPS) * gmn_ref[...]

    tm = tm_ref[...]
    cqn = _rms(zcq_ref[...], qn_ref[...]).astype(BF16)
    q_nope = _dot(cqn, wqn_ref[...]).astype(BF16)
    qa = _dot(q_nope, wbd_ref[...]) + _dot(cqn, wa_ref[...])
    qb = _dot(cqn, wb_ref[...])
    cos_m = jnp.concatenate([tm[:, :MLA_QW]] * N_HEADS, axis=1)
    sin_m = jnp.concatenate([tm[:, MLA_QW:]] * N_HEADS, axis=1)
    oqc_ref[...] = ((qa * cos_m + qb * sin_m) * MLA_SCALE).astype(BF16)
    c = _rms(zckv_ref[...], kvn_ref[...])
    kr = zkr_ref[...] * tm[:, LANES:2 * LANES] + zkrs_ref[...] * tm[:, MLA_QW + LANES:]
    olat_ref[...] = jnp.concatenate([c, kr], axis=1)
    ologf_ref[...] = _log_sigmoid(zf_ref[...] + bf_ref[...])


def _prep(z, tab_nsa, tab_mla, gm_norm, q_norm, kv_norm, fox_bf, wqn, wbd, wa, wb, tm):
    n = z.shape[0]

    def zb(off, w):
        return pl.BlockSpec((tm, w), lambda i, o=off // w: (i, o))

    def row(w):
        return pl.BlockSpec((tm, w), lambda i: (i, 0))

    def vec(w):
        return pl.BlockSpec((1, w), lambda i: (0, 0))

    z_specs = [zb(Z_NSA_Q, MIX_W), zb(Z_NSA_QS, MIX_W), zb(Z_GM_U, MIX_W), zb(Z_GM_V, MIX_W),
               zb(Z_MLA_CQ, MIX_W), zb(Z_NSA_SEL, LANES), zb(Z_NSA_WIN, LANES), zb(Z_NSA_KSS, LANES),
               zb(Z_NSA_KSW, LANES), zb(Z_NSA_G, LANES), zb(Z_MLA_CKV, LANES), zb(Z_MLA_KR, LANES),
               zb(Z_MLA_KRS, LANES), zb(Z_FOX_F, LANES)]
    qcw = N_HEADS * MLA_QW
    out_shape = (
        jax.ShapeDtypeStruct((n, MIX_W), BF16), jax.ShapeDtypeStruct((n, MIX_W), BF16),
        jax.ShapeDtypeStruct((n, LANES), F32), jax.ShapeDtypeStruct((n, LANES), F32),
        jax.ShapeDtypeStruct((n, LANES), F32),
        jax.ShapeDtypeStruct((n, MIX_W), F32), jax.ShapeDtypeStruct((n, MIX_W), F32),
        jax.ShapeDtypeStruct((n, qcw), BF16), jax.ShapeDtypeStruct((n, MLA_QW), F32),
        jax.ShapeDtypeStruct((n, LANES), F32),
    )
    out_specs = (row(MIX_W), row(MIX_W), row(LANES), row(LANES), row(LANES), row(MIX_W), row(MIX_W),
                 row(qcw), row(MLA_QW), row(LANES))
    return pl.pallas_call(
        _prep_kernel,
        out_shape=out_shape,
        grid=(n // tm,),
        in_specs=z_specs + [row(4 * LANES), row(2 * MLA_QW), vec(MIX_W), vec(MLA_DQ), vec(LANES), vec(LANES),
                            _const_spec(wqn.shape), _const_spec(wbd.shape), _const_spec(wa.shape),
                            _const_spec(wb.shape)],
        out_specs=out_specs,
        compiler_params=_cparams(("parallel",)),
        name="mixer_prep",
    )(*([z] * len(z_specs)), tab_nsa, tab_mla, gm_norm.reshape(1, -1), q_norm.reshape(1, -1),
      kv_norm.reshape(1, -1), fox_bf, wqn, wbd, wa, wb)


def _stack_heads(x, width):
    return jnp.concatenate([x[:, h * width:(h + 1) * width] for h in range(N_HEADS)], axis=0)


def _unstack_heads(x, t):
    return jnp.concatenate([x[h * t:(h + 1) * t] for h in range(N_HEADS)], axis=1)


def _tile_rows(x):
    return jnp.concatenate([x] * N_HEADS, axis=0)


def _online_update(s, v, m, l, acc):
    m_new = jnp.maximum(m, jnp.max(s, axis=-1, keepdims=True))
    alpha = jnp.exp(m - m_new)
    p = jnp.exp(s - m_new)
    l = alpha * l + jnp.sum(p, axis=-1, keepdims=True)
    acc = alpha * acc + _dot(p.astype(BF16), v)
    return m_new, l, acc


def _flash_init(rows, width):
    return (jnp.full((rows, 1), NEG, F32), jnp.zeros((rows, 1), F32), jnp.zeros((rows, width), F32))


def _gmlp_kernel(u_ref, v_ref, ws_ref, bs_ref, o_ref):
    c = GM_CHUNK
    tril = (lax.broadcasted_iota(jnp.int32, (c, c), 0) >= lax.broadcasted_iota(jnp.int32, (c, c), 1)).astype(F32)
    grp = lax.broadcasted_iota(jnp.int32, (c, MIX_W), 1) // HEAD_DIM
    ws = [(ws_ref[g] * tril).astype(BF16) for g in range(N_HEADS)]
    for ch in range(u_ref.shape[0] // c):
        v = v_ref[ch * c:(ch + 1) * c, :].astype(BF16)
        z = bs_ref[...]
        for g in range(N_HEADS):
            z = z + jnp.where(grp == g, _dot(ws[g], v), 0.0)
        o_ref[ch * c:(ch + 1) * c, :] = (u_ref[ch * c:(ch + 1) * c, :] * z).astype(o_ref.dtype)


def _gmlp_prompt(u, v, ws, bs_exp, n_rows, tq):
    return pl.pallas_call(
        _gmlp_kernel,
        out_shape=jax.ShapeDtypeStruct((n_rows, MIX_W), BF16),
        grid=(n_rows // tq,),
        in_specs=[
            pl.BlockSpec((tq, MIX_W), lambda i: (i, 0)),
            pl.BlockSpec((tq, MIX_W), lambda i: (i, 0)),
            _const_spec(ws.shape),
            _const_spec(bs_exp.shape),
        ],
        out_specs=pl.BlockSpec((tq, MIX_W), lambda i: (i, 0)),
        compiler_params=_cparams(("parallel",)),
        name="gmlp_prompt",
    )(u, v, ws, bs_exp)


def _split3(x):
    hi = x.astype(BF16)
    r = x - hi.astype(F32)
    mid = r.astype(BF16)
    lo = (r - mid.astype(F32)).astype(BF16)
    return hi, mid, lo


def _cumsum_kernel(x_ref, col_ref, row_ref, *, blk):
    tri = (lax.broadcasted_iota(jnp.int32, (blk, blk), 0)
           >= lax.broadcasted_iota(jnp.int32, (blk, blk), 1)).astype(BF16)
    carry = jnp.zeros((1, x_ref.shape[1]), F32)
    for c in range(x_ref.shape[0] // blk):
        hi, mid, lo = _split3(x_ref[c * blk:(c + 1) * blk, :])
        cs = (_dot(tri, hi) + _dot(tri, mid)) + _dot(tri, lo) + carry
        col_ref[c * blk:(c + 1) * blk, :] = cs
        row_ref[:, c * blk:(c + 1) * blk] = cs.T[:row_ref.shape[0], :]
        carry = cs[blk - 1:blk, :]


def _fox_cumsum(logf, batch, seq):
    return pl.pallas_call(
        functools.partial(_cumsum_kernel, blk=256),
        out_shape=(jax.ShapeDtypeStruct((batch * seq, LANES), F32),
                   jax.ShapeDtypeStruct((batch * N_HEADS, seq), F32)),
        grid=(batch,),
        in_specs=[pl.BlockSpec((seq, LANES), lambda b: (b, 0))],
        out_specs=(pl.BlockSpec((seq, LANES), lambda b: (b, 0)),
                   pl.BlockSpec((N_HEADS, seq), lambda b: (b, 0))),
        compiler_params=_cparams(("parallel",)),
        name="fox_cumsum",
    )(logf)


def _fox_kernel(q_ref, kv_ref, cc_ref, cr_ref, o_ref, *, tq, tk):
    i = pl.program_id(1)
    q0 = i * tq
    qs = _stack_heads((q_ref[...] * ATT_SCALE).astype(BF16), HEAD_DIM)
    cc = cc_ref[...]
    cq = jnp.concatenate([cc[:, h:h + 1] for h in range(N_HEADS)], axis=0)
    t_pos = q0 + lax.broadcasted_iota(jnp.int32, (tq, tk), 0)
    l_off = lax.broadcasted_iota(jnp.int32, (tq, tk), 1)

    def body(j, carry):
        k0 = pl.multiple_of(j * tk, tk)
        kv = kv_ref[pl.ds(k0, tk), :]
        s = _dot_nt(qs, kv[:, :HEAD_DIM].astype(BF16))
        ck = cr_ref[:, pl.ds(k0, tk)]
        ok = (k0 + l_off) <= t_pos
        s = jnp.concatenate(
            [jnp.where(ok, s[h * tq:(h + 1) * tq] - ck[h:h + 1, :], NEG) for h in range(N_HEADS)], axis=0)
        return _online_update(s + cq, kv[:, HEAD_DIM:].astype(BF16), *carry)

    nkb = (q0 + tq - 1) // tk + 1
    m, l, acc = lax.fori_loop(0, nkb, body, _flash_init(N_HEADS * tq, HEAD_DIM))
    o_ref[...] = _unstack_heads(acc / l, tq).astype(o_ref.dtype)


def _fox_prompt(z, c_col, c_row, batch, seq, tq, tk):
    nq = seq // tq
    return pl.pallas_call(
        functools.partial(_fox_kernel, tq=tq, tk=tk),
        out_shape=jax.ShapeDtypeStruct((batch * seq, MIX_W), BF16),
        grid=(batch, nq),
        in_specs=[
            pl.BlockSpec((tq, MIX_W), lambda b, i: (b * nq + i, Z_FOX_Q // MIX_W)),
            pl.BlockSpec((seq, LANES), lambda b, i: (b, Z_FOX_KV // LANES)),
            pl.BlockSpec((tq, LANES), lambda b, i: (b * nq + i, 0)),
            pl.BlockSpec((N_HEADS, seq), lambda b, i: (b, 0)),
        ],
        out_specs=pl.BlockSpec((tq, MIX_W), lambda b, i: (b * nq + i, 0)),
        compiler_params=_cparams(("parallel", "parallel")),
        name="fox_prompt",
    )(z, z, c_col, c_row)


def _mla_kernel(q_ref, lat_ref, wuv_ref, o_ref, *, tq, tk):
    i = pl.program_id(1)
    q0 = i * tq
    qs = _stack_heads(q_ref[...], MLA_QW)
    t_pos = q0 + lax.broadcasted_iota(jnp.int32, (tq, tk), 0)
    l_off = lax.broadcasted_iota(jnp.int32, (tq, tk), 1)

    def body(j, carry):
        k0 = pl.multiple_of(j * tk, tk)
        lat = lat_ref[pl.ds(k0, tk), :].astype(BF16)
        s = _dot_nt(qs, lat)
        ok = _tile_rows((k0 + l_off) <= t_pos)
        return _online_update(jnp.where(ok, s, NEG), lat[:, :MLA_DC], *carry)

    nkb = (q0 + tq - 1) // tk + 1
    m, l, acc = lax.fori_loop(0, nkb, body, _flash_init(N_HEADS * tq, MLA_DC))
    o_lat = _unstack_heads(acc / l, tq).astype(BF16)
    o_ref[...] = _dot(o_lat, wuv_ref[...]).astype(o_ref.dtype)


def _mla_prompt(qc, lat, wuv_bd, batch, seq, tq, tk):
    nq = seq // tq
    return pl.pallas_call(
        functools.partial(_mla_kernel, tq=tq, tk=tk),
        out_shape=jax.ShapeDtypeStruct((batch * seq, MIX_W), BF16),
        grid=(batch, nq),
        in_specs=[
            pl.BlockSpec((tq, N_HEADS * MLA_QW), lambda b, i: (b * nq + i, 0)),
            pl.BlockSpec((seq, MLA_QW), lambda b, i: (b, 0)),
            _const_spec(wuv_bd.shape),
        ],
        out_specs=pl.BlockSpec((tq, MIX_W), lambda b, i: (b * nq + i, 0)),
        compiler_params=_cparams(("parallel", "parallel")),
        name="mla_prompt",
    )(qc, lat, wuv_bd)


def _cmp_kernel(x_ref, w_ref, e_ref, o_ref):
    half = NSA_CMP * LANES
    for part, out in ((0, e_ref), (1, o_ref)):
        acc = jnp.zeros(out.shape, F32)
        for j in range(NSA_CMP):
            lo = part * half + j * LANES
            acc = acc + x_ref[:, lo:lo + LANES] * w_ref[j:j + 1, :]
        out[...] = acc * (1.0 / NSA_CMP)


def _nsa_compress(kv_rows, w, n_rows, tr):
    return pl.pallas_call(
        _cmp_kernel,
        out_shape=(jax.ShapeDtypeStruct((n_rows, LANES), F32), jax.ShapeDtypeStruct((n_rows, LANES), F32)),
        grid=(n_rows // tr,),
        in_specs=[pl.BlockSpec((tr, NSA_SEL * LANES), lambda i: (i, 0)), _const_spec(w.shape)],
        out_specs=(pl.BlockSpec((tr, LANES), lambda i: (i, 0)), pl.BlockSpec((tr, LANES), lambda i: (i, 0))),
        compiler_params=_cparams(("parallel",)),
        name="nsa_compress",
    )(kv_rows, w)


def _topk_mask(v, k):
    n = v.shape[1]
    idx = lax.broadcasted_iota(jnp.int32, v.shape, 1)
    rank = jnp.zeros(v.shape, F32)
    for m in range(n):
        col = v[:, m:m + 1]
        rank = rank + jnp.where(idx > m, jnp.where(col >= v, 1.0, 0.0), jnp.where(col > v, 1.0, 0.0))
    return jnp.where(rank < k, 1.0, 0.0)


def _nsa_kernel(q_ref, qr_ref, g_ref, ce_ref, co_ref, sel_ref, win_ref, o_ref, selx_scr, *, tq, tk):
    i = pl.program_id(1)
    q0 = i * tq
    seq = sel_ref.shape[0]
    ns = seq // NSA_SEL
    rows = N_HEADS * tq
    qs = _stack_heads(q_ref[...], HEAD_DIM)
    qrs = _stack_heads(qr_ref[...], HEAD_DIM)
    pos1 = q0 + lax.broadcasted_iota(jnp.int32, (tq, 1), 0)
    pos = _tile_rows(pos1)

    ce = ce_ref[...]
    co = co_ref[...]
    kc = jnp.concatenate([ce[:, :HEAD_DIM], co[:, :HEAD_DIM]], axis=0).astype(BF16)
    vc = jnp.concatenate([ce[:, HEAD_DIM:], co[:, HEAD_DIM:]], axis=0).astype(BF16)
    s = _dot_nt(qs, kc)
    n_idx = lax.broadcasted_iota(jnp.int32, (1, 2 * ns), 1)
    end_pos = jnp.where(n_idx < ns, n_idx * NSA_SEL + NSA_CMP - 1, (n_idx - ns) * NSA_SEL + NSA_SEL - 1)
    ok = end_pos <= pos
    s = jnp.where(ok, s, NEG)
    e = jnp.exp(s - jnp.max(s, axis=-1, keepdims=True))
    p = e / jnp.sum(e, axis=-1, keepdims=True) * jnp.where(ok, 1.0, 0.0)
    o_cmp = _dot(p.astype(BF16), vc)

    ph = p[0:tq]
    for h in range(1, N_HEADS):
        ph = ph + p[h * tq:(h + 1) * tq]
    imp = ph[:, :ns] + ph[:, ns:]
    blk = lax.broadcasted_iota(jnp.int32, (tq, ns), 1)
    cur = pos1 // NSA_SEL
    imp = jnp.where(blk > cur, NEG, imp)
    for forced in (0, cur, cur - 1):
        imp = jnp.where(blk == forced, FORCE, imp)
    sel = _topk_mask(imp, NSA_TOPK).astype(BF16)
    expand = (lax.broadcasted_iota(jnp.int32, (ns, seq), 1) // NSA_SEL
              == lax.broadcasted_iota(jnp.int32, (ns, seq), 0)).astype(BF16)
    selx_scr[...] = _dot(sel, expand)

    t_pos = q0 + lax.broadcasted_iota(jnp.int32, (tq, tk), 0)
    l_off = lax.broadcasted_iota(jnp.int32, (tq, tk), 1)

    def sel_body(j, carry):
        k0 = pl.multiple_of(j * tk, tk)
        kv = sel_ref[pl.ds(k0, tk), :]
        s = _dot_nt(qrs, kv[:, :HEAD_DIM].astype(BF16))
        ok = _tile_rows(((k0 + l_off) <= t_pos) & (selx_scr[:, pl.ds(k0, tk)] > 0.5))
        return _online_update(jnp.where(ok, s, NEG), kv[:, HEAD_DIM:].astype(BF16), *carry)

    nkb = (q0 + tq - 1) // tk + 1
    _, l_sel, a_sel = lax.fori_loop(0, nkb, sel_body, _flash_init(rows, HEAD_DIM))

    def win_body(j, carry):
        k0 = pl.multiple_of(j * tk, tk)
        kv = win_ref[pl.ds(k0, tk), :]
        s = _dot_nt(qrs, kv[:, :HEAD_DIM].astype(BF16))
        dist = t_pos - (k0 + l_off)
        ok = _tile_rows((dist >= 0) & (dist <= NSA_WINDOW))
        return _online_update(jnp.where(ok, s, NEG), kv[:, HEAD_DIM:].astype(BF16), *carry)

    first = jnp.maximum(q0 - NSA_WINDOW, 0) // tk
    _, l_win, a_win = lax.fori_loop(first, nkb, win_body, _flash_init(rows, HEAD_DIM))

    g = g_ref[...]
    gate = [jnp.concatenate([g[:, 3 * h + c:3 * h + c + 1] for h in range(N_HEADS)], axis=0) for c in range(3)]
    o = gate[0] * o_cmp + gate[1] * (a_sel / l_sel) + gate[2] * (a_win / l_win)
    o_ref[...] = _unstack_heads(o, tq).astype(o_ref.dtype)


def _nsa_prompt(q, qr, gates, cmp_e, cmp_o, kv_sel, kv_win, batch, seq, tq, tk):
    nq = seq // tq
    ns = seq // NSA_SEL

    def qspec(w):
        return pl.BlockSpec((tq, w), lambda b, i: (b * nq + i, 0))

    def bspec(r, w):
        return pl.BlockSpec((r, w), lambda b, i: (b, 0))

    return pl.pallas_call(
        functools.partial(_nsa_kernel, tq=tq, tk=tk),
        out_shape=jax.ShapeDtypeStruct((batch * seq, MIX_W), BF16),
        grid=(batch, nq),
        in_specs=[qspec(MIX_W), qspec(MIX_W), qspec(LANES), bspec(ns, LANES), bspec(ns, LANES),
                  bspec(seq, LANES), bspec(seq, LANES)],
        out_specs=qspec(MIX_W),
        scratch_shapes=[pltpu.VMEM((tq, seq), F32)],
        compiler_params=_cparams(("parallel", "parallel")),
        name="nsa_prompt",
    )(q, qr, gates, cmp_e, cmp_o, kv_sel, kv_win)


def _page_dma(cache_ref, pt_ref, b, base, dst_fn, sem, n_pages, start):
    def body(j, c):
        cp = pltpu.make_async_copy(cache_ref.at[pt_ref[b, j] + base], dst_fn(j), sem)
        if start:
            cp.start()
        else:
            cp.wait()
        return c

    lax.fori_loop(0, n_pages, body, 0)


def _prefetch_schedule(step, n_steps, issue):
    slot = step % 2

    @pl.when(step == 0)
    def _():
        issue(step, slot, True)

    @pl.when(step + 1 < n_steps)
    def _():
        issue(step + 1, 1 - slot, True)

    issue(step, slot, False)
    return slot


def _softmax_parts(parts):
    m = parts[0].max(axis=-1, keepdims=True)
    for s in parts[1:]:
        m = jnp.maximum(m, s.max(axis=-1, keepdims=True))
    es = [jnp.exp(s - m) for s in parts]
    l = es[0].sum(axis=-1, keepdims=True)
    for e in es[1:]:
        l = l + e.sum(axis=-1, keepdims=True)
    return es, l


def _rowdot(q, k_row):
    return jnp.sum(q.astype(F32) * k_row.astype(BF16).astype(F32), axis=-1, keepdims=True)


def _nsa_s1_kernel(pt_ref, cache_ref, q_ref, new_ref, w_ref, imp_ref, o_ref, buf, sem, *, base, past):
    b = pl.program_id(0)
    n_pages = past // PAGE_SIZE
    ns = past // NSA_SEL

    def issue(step, slot, start):
        _page_dma(cache_ref, pt_ref, step, base,
                  lambda j: buf.at[slot, pl.ds(pl.multiple_of(j * PAGE_SIZE, PAGE_SIZE), PAGE_SIZE)],
                  sem.at[slot], n_pages, start)

    slot = _prefetch_schedule(b, pl.num_programs(0), issue)

    acc_e = jnp.zeros((ns, LANES), F32)
    acc_o = jnp.zeros((ns, LANES), F32)
    for j in range(NSA_CMP):
        wj = w_ref[j:j + 1, :]
        acc_e = acc_e + buf[slot, pl.ds(j, ns, stride=NSA_SEL), :] * wj
        acc_o = acc_o + buf[slot, pl.ds(NSA_CMP + j, ns, stride=NSA_SEL), :] * wj
    ce = (acc_e * (1.0 / NSA_CMP)).astype(BF16)
    co = (acc_o * (1.0 / NSA_CMP)).astype(BF16)
    te = new_ref[0] * w_ref[0:1, :] * (1.0 / NSA_CMP)
    to = jnp.zeros_like(te)

    q = q_ref[0]
    blk = lax.broadcasted_iota(jnp.int32, (1, ns), 1)
    ok_e = blk * NSA_SEL + NSA_CMP - 1 <= past
    ok_o = blk * NSA_SEL + NSA_SEL - 1 <= past
    ok_te = ns * NSA_SEL + NSA_CMP - 1 <= past
    ok_to = ns * NSA_SEL + NSA_SEL - 1 <= past
    s_e = jnp.where(ok_e, _dot_nt(q, ce), NEG)
    s_o = jnp.where(ok_o, _dot_nt(q, co), NEG)
    s_te = jnp.where(ok_te, _rowdot(q, te), NEG)
    s_to = jnp.where(ok_to, _rowdot(q, to), NEG)
    (e_e, e_o, e_te, e_to), l = _softmax_parts([s_e, s_o, s_te, s_to])
    p_e = e_e / l * jnp.where(ok_e, 1.0, 0.0)
    p_o = e_o / l * jnp.where(ok_o, 1.0, 0.0)
    p_te = e_te / l * (1.0 if ok_te else 0.0)
    p_to = e_to / l * (1.0 if ok_to else 0.0)
    o = _dot(p_e.astype(BF16), ce) + _dot(p_o.astype(BF16), co) + p_te * te + p_to * to
    o_ref[0] = o[:, HEAD_DIM:]
    imp_past = jnp.sum(p_e + p_o, axis=0, keepdims=True)
    imp_tail = jnp.sum(p_te + p_to, axis=0, keepdims=True)
    imp_ref[0] = jnp.concatenate([imp_past, jnp.broadcast_to(imp_tail, (1, LANES))], axis=1)


def _nsa_sample_cmp(page_table, cache, q8, kv_new, w, base, past):
    nb = q8.shape[0]
    ns = past // NSA_SEL
    grid_spec = pltpu.PrefetchScalarGridSpec(
        num_scalar_prefetch=1,
        grid=(nb,),
        in_specs=[
            pl.BlockSpec(memory_space=pl.ANY),
            pl.BlockSpec((1, N_HEADS, LANES), lambda b, pt: (b, 0, 0)),
            pl.BlockSpec((1, 1, LANES), lambda b, pt: (b, 0, 0)),
            pl.BlockSpec(w.shape, lambda b, pt: (0, 0)),
        ],
        out_specs=(pl.BlockSpec((1, 1, ns + LANES), lambda b, pt: (b, 0, 0)),
                   pl.BlockSpec((1, N_HEADS, HEAD_DIM), lambda b, pt: (b, 0, 0))),
        scratch_shapes=[pltpu.VMEM((2, past, LANES), F32), pltpu.SemaphoreType.DMA((2,))],
    )
    return pl.pallas_call(
        functools.partial(_nsa_s1_kernel, base=base, past=past),
        out_shape=(jax.ShapeDtypeStruct((nb, 1, ns + LANES), F32),
                   jax.ShapeDtypeStruct((nb, N_HEADS, HEAD_DIM), F32)),
        grid_spec=grid_spec,
        compiler_params=_cparams(("arbitrary",)),
        name="nsa_sample_cmp",
    )(page_table, cache, q8, kv_new, w)


def _topk_kernel(imp_ref, idx_ref, *, cur, k):
    x = imp_ref[...]
    lane = lax.broadcasted_iota(jnp.int32, x.shape, 1)
    x = jnp.where((lane == 0) | (lane == cur) | (lane == cur - 1), FORCE, jnp.where(lane > cur, NEG, x))
    lane_f = lane.astype(F32)
    out_lane = lax.broadcasted_iota(jnp.int32, idx_ref.shape, 1)
    out = jnp.zeros(idx_ref.shape, F32)
    for i in range(k):
        m = jnp.max(x, axis=-1, keepdims=True)
        pick = jnp.min(jnp.where(x == m, lane_f, 1e9), axis=-1, keepdims=True)
        out = jnp.where(out_lane == i, pick, out)
        x = jnp.where(lane_f == pick, -3.0e38, x)
    idx_ref[...] = out.astype(jnp.int32)


def _nsa_sample_topk(imp, cur):
    nb, w = imp.shape
    return pl.pallas_call(
        functools.partial(_topk_kernel, cur=cur, k=min(NSA_TOPK, cur + 1)),
        out_shape=jax.ShapeDtypeStruct((nb, LANES), jnp.int32),
        in_specs=[pl.BlockSpec((nb, w), lambda: (0, 0))],
        out_specs=pl.BlockSpec((nb, LANES), lambda: (0, 0)),
        name="nsa_sample_topk",
    )(imp)


def _nsa_s2_kernel(pt_ref, idx_ref, cache_ref, qr_ref, seln_ref, winn_ref, winb_ref, g_ref, ocmp_ref, o_ref,
                   buf, sem, *, base, past):
    b = pl.program_id(0)
    n_past = past // NSA_SEL
    per_page = PAGE_SIZE // NSA_SEL
    nk = NSA_TOPK * NSA_SEL

    def issue(step, slot, start):
        for k in range(NSA_TOPK):
            n = jnp.minimum(idx_ref[step, k], n_past - 1)
            page = pt_ref[step, n // per_page] + base
            row = pl.multiple_of((n % per_page) * NSA_SEL, NSA_SEL)
            cp = pltpu.make_async_copy(cache_ref.at[page, pl.ds(row, NSA_SEL)],
                                       buf.at[slot, pl.ds(k * NSA_SEL, NSA_SEL)], sem.at[slot])
            if start:
                cp.start()
            else:
                cp.wait()

    slot = _prefetch_schedule(b, pl.num_programs(0), issue)

    qr = qr_ref[0]
    lane = lax.broadcasted_iota(jnp.int32, (1, nk), 1)
    tok = jnp.zeros((1, nk), jnp.int32)
    is_past = jnp.zeros((1, nk), jnp.int32)
    tail_ok = jnp.int32(0)
    for k in range(NSA_TOPK):
        ik = idx_ref[b, k]
        in_k = lane // NSA_SEL == k
        tok = jnp.where(in_k, ik * NSA_SEL + lane % NSA_SEL, tok)
        is_past = jnp.where(in_k, (ik < n_past).astype(jnp.int32), is_past)
        tail_ok = tail_ok | ((ik >= n_past) & (ik * NSA_SEL <= past)).astype(jnp.int32)
    kv = buf[slot].astype(BF16)
    s_sel = jnp.where((is_past > 0) & (tok <= past), _dot_nt(qr, kv), NEG)
    s_new = jnp.where(tail_ok > 0, _rowdot(qr, seln_ref[0]), NEG)
    (e_sel, e_new), l = _softmax_parts([s_sel, s_new])
    o_sel = (_dot(e_sel.astype(BF16), kv) + e_new * seln_ref[0]) / l

    wb = winb_ref.shape[1]
    kvw = winb_ref[0].astype(BF16)
    dist = wb - lax.broadcasted_iota(jnp.int32, (1, wb), 1)
    s_win = jnp.where((dist >= 0) & (dist <= NSA_WINDOW) & (past - dist >= 0), _dot_nt(qr, kvw), NEG)
    s_wn = _rowdot(qr, winn_ref[0])
    (e_win, e_wn), lw = _softmax_parts([s_win, s_wn])
    o_win = (_dot(e_win.astype(BF16), kvw) + e_wn * winn_ref[0]) / lw

    g = g_ref[0]
    o = g[:, 0:1] * ocmp_ref[0] + g[:, 1:2] * o_sel[:, HEAD_DIM:] + g[:, 2:3] * o_win[:, HEAD_DIM:]
    o_ref[0] = o.astype(o_ref.dtype)


def _nsa_sample_attend(page_table, idx, cache, qr8, sel_new, win_new, win_buf, gates8, o_cmp, base, win_base, past):
    nb = qr8.shape[0]
    wb = win_buf.shape[1]

    def per_b(r, w):
        return pl.BlockSpec((1, r, w), lambda b, pt, ix: (b, 0, 0))

    grid_spec = pltpu.PrefetchScalarGridSpec(
        num_scalar_prefetch=2,
        grid=(nb,),
        in_specs=[
            pl.BlockSpec(memory_space=pl.ANY),
            per_b(N_HEADS, LANES), per_b(1, LANES), per_b(1, LANES),
            pl.BlockSpec((1, wb, LANES), lambda b, pt, ix: (win_base + b, 0, 0)),
            per_b(N_HEADS, LANES), per_b(N_HEADS, HEAD_DIM),
        ],
        out_specs=per_b(N_HEADS, HEAD_DIM),
        scratch_shapes=[pltpu.VMEM((2, NSA_TOPK * NSA_SEL, LANES), F32), pltpu.SemaphoreType.DMA((2,))],
    )
    return pl.pallas_call(
        functools.partial(_nsa_s2_kernel, base=base, past=past),
        out_shape=jax.ShapeDtypeStruct((nb, N_HEADS, HEAD_DIM), BF16),
        grid_spec=grid_spec,
        compiler_params=_cparams(("arbitrary",)),
        name="nsa_sample_attend",
    )(page_table, idx, cache, qr8, sel_new, win_new, win_buf, gates8, o_cmp)


def _mla_s_kernel(pt_ref, cache_ref, q_ref, new_ref, wuv_ref, o_ref, buf, sem, m_scr, l_scr, acc_scr,
                  *, base, past, pch):
    b = pl.program_id(0)
    c = pl.program_id(1)
    nch = pl.num_programs(1)
    step = b * nch + c
    rows = pch * PAGE_SIZE
    width = cache_ref.shape[2]

    def issue(st, slot, start):
        sb = st // nch
        sc = st % nch

        def body(j, carry):
            page = pt_ref[sb, sc * pch + j] + base
            dst = buf.at[slot, pl.ds(pl.multiple_of(j * PAGE_SIZE, PAGE_SIZE), PAGE_SIZE)]
            cp = pltpu.make_async_copy(cache_ref.at[page], dst, sem.at[slot])
            if start:
                cp.start()
            else:
                cp.wait()
            return carry

        lax.fori_loop(0, pch, body, 0)

    slot = _prefetch_schedule(step, pl.num_programs(0) * nch, issue)

    @pl.when(c == 0)
    def _():
        m_scr[...] = jnp.full_like(m_scr, NEG)
        l_scr[...] = jnp.zeros_like(l_scr)
        acc_scr[...] = jnp.zeros_like(acc_scr)

    q = q_ref[0][:, :width]
    lat = buf[slot].astype(BF16)
    k_pos = c * rows + lax.broadcasted_iota(jnp.int32, (1, rows), 1)
    s = jnp.where(k_pos <= past, _dot_nt(q, lat), NEG)
    m, l, acc = _online_update(s, lat, m_scr[...], l_scr[...], acc_scr[...])
    m_scr[...] = m
    l_scr[...] = l
    acc_scr[...] = acc

    @pl.when(c == nch - 1)
    def _():
        new = new_ref[0][:, :width]
        s_new = _rowdot(q, new)
        m2 = jnp.maximum(m, s_new)
        alpha = jnp.exp(m - m2)
        e_new = jnp.exp(s_new - m2)
        o_lat = ((alpha * acc + e_new * new) / (alpha * l + e_new))[:, :MLA_DC].astype(BF16)
        full = _dot(o_lat, wuv_ref[...])
        head = lax.broadcasted_iota(jnp.int32, (N_HEADS, HEAD_DIM), 0)
        o = jnp.zeros((N_HEADS, HEAD_DIM), F32)
        for h in range(N_HEADS):
            o = jnp.where(head == h, full[:, h * HEAD_DIM:(h + 1) * HEAD_DIM], o)
        o_ref[0] = o.astype(o_ref.dtype)


def _mla_sample(page_table, cache, qc8, lat_new, wuv, base, past, pch):
    nb = qc8.shape[0]
    n_pages = past // PAGE_SIZE
    grid_spec = pltpu.PrefetchScalarGridSpec(
        num_scalar_prefetch=1,
        grid=(nb, n_pages // pch),
        in_specs=[
            pl.BlockSpec(memory_space=pl.ANY),
            pl.BlockSpec((1, N_HEADS, MLA_QW), lambda b, c, pt: (b, 0, 0)),
            pl.BlockSpec((1, 1, MLA_QW), lambda b, c, pt: (b, 0, 0)),
            pl.BlockSpec(wuv.shape, lambda b, c, pt: (0, 0)),
        ],
        out_specs=pl.BlockSpec((1, N_HEADS, HEAD_DIM), lambda b, c, pt: (b, 0, 0)),
        scratch_shapes=[pltpu.VMEM((2, pch * PAGE_SIZE, cache.shape[2]), F32), pltpu.SemaphoreType.DMA((2,)),
                        pltpu.VMEM((N_HEADS, 1), F32), pltpu.VMEM((N_HEADS, 1), F32),
                        pltpu.VMEM((N_HEADS, cache.shape[2]), F32)],
    )
    return pl.pallas_call(
        functools.partial(_mla_s_kernel, base=base, past=past, pch=pch),
        out_shape=jax.ShapeDtypeStruct((nb, N_HEADS, HEAD_DIM), BF16),
        grid_spec=grid_spec,
        compiler_params=_cparams(("arbitrary", "arbitrary")),
        name="mla_sample",
    )(page_table, cache, qc8, lat_new, wuv)


def _dot_split(x, w, parts):
    pieces = _split3(x)[:parts]
    out = _dot(pieces[0], w)
    for p in pieces[1:]:
        out = out + _dot(p, w)
    return out


def _fox_bias_kernel(pt_ref, cache_ref, u_ref, v_ref, e_ref, o_ref, buf, sem, *, base, past, group):
    i = pl.program_id(0)
    n_pages = past // PAGE_SIZE

    def issue(step, slot, start):
        for g in range(group):
            _page_dma(cache_ref, pt_ref, step * group + g, base,
                      lambda j, g=g: buf.at[slot, pl.ds(g * n_pages + j, 1)],
                      sem.at[slot], n_pages, start)

    slot = _prefetch_schedule(i, pl.num_programs(0), issue)
    x = buf[slot]
    within = _dot_split(x, u_ref[...], 3)
    tot = _dot_split(x, v_ref[...], 3)
    upper = (lax.broadcasted_iota(jnp.int32, (n_pages, n_pages), 1)
             > lax.broadcasted_iota(jnp.int32, (n_pages, n_pages), 0)).astype(BF16)
    later = jnp.concatenate(
        [_dot_split_lhs(upper, tot[g * n_pages:(g + 1) * n_pages]) for g in range(group)], axis=0)
    o_ref[...] = within + _dot_split(later, e_ref[...], 3)


def _dot_split_lhs(w, x):
    hi, mid, lo = _split3(x)
    return (_dot(w, hi) + _dot(w, mid)) + _dot(w, lo)


def _fox_sample_bias(page_table, cache, u, v, e, base, past, group):
    nb = page_table.shape[0]
    n_pages = past // PAGE_SIZE
    pw = cache.shape[2]
    grid_spec = pltpu.PrefetchScalarGridSpec(
        num_scalar_prefetch=1,
        grid=(nb // group,),
        in_specs=[
            pl.BlockSpec(memory_space=pl.ANY),
            pl.BlockSpec(u.shape, lambda i, pt: (0, 0)),
            pl.BlockSpec(v.shape, lambda i, pt: (0, 0)),
            pl.BlockSpec(e.shape, lambda i, pt: (0, 0)),
        ],
        out_specs=pl.BlockSpec((group * n_pages, pw), lambda i, pt: (i, 0)),
        scratch_shapes=[pltpu.VMEM((2, group * n_pages, pw), F32), pltpu.SemaphoreType.DMA((2,))],
    )
    return pl.pallas_call(
        functools.partial(_fox_bias_kernel, base=base, past=past, group=group),
        out_shape=jax.ShapeDtypeStruct((nb * n_pages, pw), F32),
        grid_spec=grid_spec,
        compiler_params=_cparams(("arbitrary",)),
        name="fox_sample_bias",
    )(page_table, cache, u, v, e)


def _fox_s_kernel(pt_ref, cache_ref, q_ref, bias_ref, new_ref, lnew_ref, o_ref, buf, sem, *, base, past):
    b = pl.program_id(0)
    n_pages = past // PAGE_SIZE

    def issue(step, slot, start):
        _page_dma(cache_ref, pt_ref, step, base,
                  lambda j: buf.at[slot, pl.ds(pl.multiple_of(j * PAGE_SIZE, PAGE_SIZE), PAGE_SIZE)],
                  sem.at[slot], n_pages, start)

    slot = _prefetch_schedule(b, pl.num_programs(0), issue)
    q = (q_ref[0] * ATT_SCALE).astype(BF16)
    kv = buf[slot].astype(BF16)
    bias =jnp.concatenate([bias_ref[j] for j in range(n_pages)], axis=1)
    k_pos = lax.broadcasted_iota(jnp.int32, (1, past), 1)
    s = jnp.where(k_pos <= past, _dot_nt(q, kv) + bias + lnew_ref[0][:, 0:1], NEG)
    s_new = _rowdot(q, new_ref[0])
    (e, e_new), l = _softmax_parts([s, s_new])
    o = (_dot(e.astype(BF16), kv) + e_new * new_ref[0]) / l
    o_ref[0] = o[:, HEAD_DIM:].astype(o_ref.dtype)


def _fox_sample(page_table, cache, q8, bias, kv_new, lnew, base, past):
    nb = q8.shape[0]
    n_pages = past // PAGE_SIZE

    def per_b(r, w):
        return pl.BlockSpec((1, r, w), lambda b, pt: (b, 0, 0))

    grid_spec = pltpu.PrefetchScalarGridSpec(
        num_scalar_prefetch=1,
        grid=(nb,),
        in_specs=[
            pl.BlockSpec(memory_space=pl.ANY),
            per_b(N_HEADS, LANES),
            pl.BlockSpec((n_pages, N_HEADS, PAGE_SIZE), lambda b, pt: (b, 0, 0)),
            per_b(1, LANES), per_b(N_HEADS, LANES),
        ],
        out_specs=per_b(N_HEADS, HEAD_DIM),
        scratch_shapes=[pltpu.VMEM((2, past, LANES), F32), pltpu.SemaphoreType.DMA((2,))],
    )
    return pl.pallas_call(
        functools.partial(_fox_s_kernel, base=base, past=past),
        out_shape=jax.ShapeDtypeStruct((nb, N_HEADS, HEAD_DIM), BF16),
        grid_spec=grid_spec,
        compiler_params=_cparams(("arbitrary",)),
        name="fox_sample",
    )(page_table, cache, q8, bias, kv_new, lnew)


def _gmlp_s_kernel(u_ref, v_ref, w_ref, b_ref, o_ref):
    o_ref[...] = (u_ref[...] * (v_ref[...] * w_ref[...] + b_ref[...])).astype(o_ref.dtype)


def _gmlp_sample(u, v, w00, b0, row0, nb):
    blk = row0 // nb
    return pl.pallas_call(
        _gmlp_s_kernel,
        out_shape=jax.ShapeDtypeStruct((nb, MIX_W), BF16),
        grid=(1,),
        in_specs=[pl.BlockSpec((nb, MIX_W), lambda i: (blk, 0)), pl.BlockSpec((nb, MIX_W), lambda i: (blk, 0)),
                  pl.BlockSpec((1, MIX_W), lambda i: (0, 0)), pl.BlockSpec((1, MIX_W), lambda i: (0, 0))],
        out_specs=pl.BlockSpec((nb, MIX_W), lambda i: (0, 0)),
        name="gmlp_sample",
    )(u, v, w00, b0)


def _swap_rot(w, rot):
    half = rot // 2
    return jnp.concatenate([-w[..., half:rot], w[..., :half], jnp.zeros_like(w[..., rot:])], axis=-1)


def _pad_last(w, n):
    return jnp.pad(w, [(0, 0)] * (w.ndim - 1) + [(0, n - w.shape[-1])])


def _small_proj_weight(w_in):
    d = w_in.shape[0]
    off = [0]

    def take(n):
        s = w_in[:, off[0]:off[0] + n]
        off[0] += n
        return s

    nsa_q = take(MIX_W)
    nsa_kv = take(6 * HEAD_DIM)
    nsa_g = take(3 * N_HEADS)
    gm_u, gm_v, mla_cq = take(MIX_W), take(MIX_W), take(MLA_DQ)
    mla_ckv, mla_kr = take(MLA_DC), take(MLA_DR)
    fox_q, fox_kv, fox_f = take(MIX_W), take(2 * HEAD_DIM), take(N_HEADS)
    nsa_qs = _swap_rot(nsa_q.reshape(d, N_HEADS, HEAD_DIM), ROT_DIM).reshape(d, MIX_W)
    sel_ks = _swap_rot(nsa_kv[:, 2 * HEAD_DIM:3 * HEAD_DIM], ROT_DIM)
    win_ks = _swap_rot(nsa_kv[:, 4 * HEAD_DIM:5 * HEAD_DIM], ROT_DIM)
    segs = [nsa_q, nsa_qs, gm_u, gm_v, mla_cq, fox_q,
            nsa_kv[:, :LANES], nsa_kv[:, LANES:2 * LANES], nsa_kv[:, 2 * LANES:],
            _pad_last(sel_ks, LANES), _pad_last(win_ks, LANES), _pad_last(nsa_g, LANES),
            mla_ckv, _pad_last(mla_kr, LANES), _pad_last(_swap_rot(mla_kr, MLA_DR), LANES),
            fox_kv, _pad_last(fox_f, LANES)]
    w_small = jnp.concatenate(segs, axis=1).astype(BF16)
    assert w_small.shape[1] == Z_W
    return w_small, w_in[:, off[0]:].astype(BF16)


def _block_diag(w):
    h, r, c = w.shape
    eye = jnp.eye(h, dtype=bool)
    return jnp.where(eye[:, None, :, None], w[:, :, None, :], 0).reshape(h * r, h * c)


def _mla_weights(wq, wuk, wuv):
    dq = wq.shape[0]
    wq = wq.reshape(dq, N_HEADS, HEAD_DIM + MLA_DR)
    rope = wq[:, :, HEAD_DIM:]
    lead = jnp.zeros((dq, N_HEADS, MLA_DC), wq.dtype)
    wa = _pad_last(jnp.concatenate([lead, rope], axis=-1), MLA_QW).reshape(dq, N_HEADS * MLA_QW)
    wb = _pad_last(jnp.concatenate([lead, _swap_rot(rope, MLA_DR)], axis=-1), MLA_QW).reshape(dq, N_HEADS * MLA_QW)
    wqn = wq[:, :, :HEAD_DIM].reshape(dq, MIX_W)
    wbd = _block_diag(_pad_last(jnp.transpose(wuk, (1, 2, 0)), MLA_QW))
    wuv_bd = _block_diag(jnp.transpose(wuv, (1, 0, 2)))
    return (wqn.astype(BF16), wbd.astype(BF16), wa.astype(BF16), wb.astype(BF16), wuv_bd.astype(BF16),
            wuv.reshape(MLA_DC, MIX_W).astype(BF16))


def _rope_tables(pos):
    def cs(half, theta):
        inv = theta ** (-jnp.arange(half, dtype=F32) / half)
        ang = pos.astype(F32)[:, None] * inv[None, :]
        return jnp.cos(ang), jnp.sin(ang)

    n = pos.shape[0]
    c, s = cs(ROT_DIM // 2, ROPE_THETA)
    one, zero = jnp.ones((n, HEAD_DIM - ROT_DIM), F32), jnp.zeros((n, HEAD_DIM - ROT_DIM), F32)
    hc = jnp.concatenate([c, c, one], axis=1)
    hs = jnp.concatenate([s, s, zero], axis=1)
    tab_nsa = jnp.concatenate([hc, hc, hs, hs, hc, jnp.ones((n, HEAD_DIM), F32), hs, jnp.zeros((n, HEAD_DIM), F32)],
                              axis=1)
    c, s = cs(MLA_DR // 2, MLA_THETA)
    tail = jnp.zeros((n, MLA_QW - MLA_DC - MLA_DR), F32)
    tab_mla = jnp.concatenate([jnp.ones((n, MLA_DC), F32), c, c, tail, jnp.zeros((n, MLA_DC), F32), s, s, tail], axis=1)
    return tab_nsa, tab_mla


def _fox_bias_consts():
    t1, h1 = np.divmod(np.arange(PAGE_SIZE * N_HEADS), N_HEADS)
    h2, t2 = np.divmod(np.arange(PAGE_SIZE * N_HEADS), PAGE_SIZE)
    u = (h1[:, None] == h2[None, :]) & (t1[:, None] > t2[None, :])
    copies = LANES // N_HEADS
    hv, rv = np.divmod(np.arange(LANES), copies)
    v = h1[:, None] == hv[None, :]
    e = (hv[:, None] == h2[None, :]) & (rv[:, None] == 0)
    return tuple(jnp.asarray(a, dtype=BF16) for a in (u, v, e))


def kernel(x_prompt, x_sample, cache_nsa_cmp, cache_nsa_sel, state_nsa_win, cache_mla, cache_fox_kv, cache_fox_logf, page_table, p_prompt, p_sample, n_ffn1, ffn1_in, ffn1_out, n_mix, w_in, nsa_cmp_w, gm_norm, gm_ws, gm_bs, mla_q_norm, mla_wq, mla_kv_norm, mla_wuk, mla_wuv, fox_bf, w_br, w_o, n_ffn2, ffn2_in, ffn2_out, n_ple, ple_wg, ple_wp, norm_final):
    batch, seq, d = x_prompt.shape
    nb = x_sample.shape[0]
    depth = w_in.shape[0]
    n_pool = cache_mla.shape[1]
    past = page_table.shape[1] * PAGE_SIZE
    wb = state_nsa_win.shape[2]
    n_p = batch * seq
    n = n_p + nb
    tm = 384 if n % 384 == 0 else 128
    tq, tk = 128, 256

    x = jnp.concatenate([x_prompt.reshape(n_p, d), x_sample.reshape(nb, d)], axis=0)
    pos = jnp.concatenate([jnp.arange(n_p, dtype=jnp.int32) % seq, jnp.full((nb,), past, jnp.int32)])
    tab_nsa, tab_mla = _rope_tables(pos)
    bias_u, bias_v, bias_e = _fox_bias_consts()
    c_nsa_cmp = cache_nsa_cmp.reshape(depth * n_pool, PAGE_SIZE, LANES)
    c_nsa_sel = cache_nsa_sel.reshape(depth * n_pool, PAGE_SIZE, LANES)
    c_mla = cache_mla.reshape(depth * n_pool, PAGE_SIZE, MLA_DC + MLA_DR)
    c_fox_kv = cache_fox_kv.reshape(depth * n_pool, PAGE_SIZE, LANES)
    c_fox_logf = cache_fox_logf.reshape(depth * n_pool, 1, PAGE_SIZE * N_HEADS)
    win_state = state_nsa_win.reshape(depth * nb, wb, LANES)

    st_p, st_s = [], []
    for l in range(depth):
        w_small, w_gate = _small_proj_weight(w_in[l])
        wqn, wbd, wa, wb_, wuv_bd, wuv_flat = _mla_weights(mla_wq[l], mla_wuk[l], mla_wuv[l])
        cmp_w = nsa_cmp_w[l].reshape(NSA_CMP, LANES)
        bs_exp = jnp.repeat(gm_bs[l].T, HEAD_DIM, axis=1)
        w00 = jnp.repeat(gm_ws[l][:, 0, 0], HEAD_DIM).reshape(1, MIX_W)
        b0 = jnp.repeat(gm_bs[l][:, 0], HEAD_DIM).reshape(1, MIX_W)

        x = _ffn(x, n_ffn1[l], ffn1_in[l].astype(BF16), ffn1_out[l].astype(BF16), tm, 512)
        z = _proj(x, n_mix[l], w_small, tm)
        q, qr, kv_sel, kv_win, gates, gm_u, gm_v, qc, lat, logf = _prep(
            z, tab_nsa, tab_mla, gm_norm[l], mla_q_norm[l], mla_kv_norm[l],
            _pad_last(fox_bf[l].reshape(1, N_HEADS), LANES), wqn, wbd, wa, wb_, tm)
        kv_cmp = z[:, Z_NSA_CMP:Z_NSA_CMP + LANES]
        fox_kv = z[:, Z_FOX_KV:Z_FOX_KV + LANES]

        cmp_e, cmp_o = _nsa_compress(kv_cmp.reshape(n // NSA_SEL, NSA_SEL * LANES), cmp_w,
                                     n_p // NSA_SEL, seq // NSA_SEL)
        o_nsa = _nsa_prompt(q, qr, gates, cmp_e, cmp_o, kv_sel, kv_win, batch, seq, tq, tk)
        o_gm = _gmlp_prompt(gm_u, gm_v, gm_ws[l], bs_exp, n_p, 512)
        o_mla = _mla_prompt(qc, lat, wuv_bd, batch, seq, tq, tk)
        c_col, c_row = _fox_cumsum(logf, batch, seq)
        o_fox = _fox_prompt(z, c_col, c_row, batch, seq, tq, tk)

        base = l * n_pool
        q8 = _pad_last(q[n_p:].reshape(nb, N_HEADS, HEAD_DIM), LANES)
        qr8 = _pad_last(qr[n_p:].reshape(nb, N_HEADS, HEAD_DIM), LANES)
        imp, o_cmp = _nsa_sample_cmp(page_table, c_nsa_cmp, q8, kv_cmp[n_p:].reshape(nb, 1, LANES), cmp_w, base, past)
        idx = _nsa_sample_topk(imp.reshape(nb, -1), past // NSA_SEL)
        gates8 = _pad_last(gates[n_p:, :3 * N_HEADS].reshape(nb, N_HEADS, 3), LANES)
        o_nsa_s = _nsa_sample_attend(page_table, idx, c_nsa_sel, qr8, kv_sel[n_p:].reshape(nb, 1, LANES),
                                     kv_win[n_p:].reshape(nb, 1, LANES), win_state, gates8, o_cmp,
                                     base, l * nb, past)
        o_gm_s = _gmlp_sample(gm_u, gm_v, w00, b0, n_p, nb)
        o_mla_s = _mla_sample(page_table, c_mla, qc[n_p:].reshape(nb, N_HEADS, MLA_QW),
                              lat[n_p:].reshape(nb, 1, MLA_QW), wuv_flat, base, past, min(32, past // PAGE_SIZE))
        bias = _fox_sample_bias(page_table, c_fox_logf, bias_u, bias_v, bias_e, base, past, 8)
        fq8 = _pad_last(z[n_p:, Z_FOX_Q:Z_FOX_Q + MIX_W].reshape(nb, N_HEADS, HEAD_DIM), LANES)
        lnew = jnp.broadcast_to(logf[n_p:, :N_HEADS, None], (nb, N_HEADS, LANES))
        o_fox_s = _fox_sample(page_table, c_fox_kv, fq8, bias.reshape(-1, N_HEADS, PAGE_SIZE),
                              fox_kv[n_p:].reshape(nb, 1, LANES), lnew, base, past)

        outs = [jnp.concatenate([a, b.reshape(nb, MIX_W)], axis=0)
                for a, b in ((o_nsa, o_nsa_s), (o_gm, o_gm_s), (o_mla, o_mla_s), (o_fox, o_fox_s))]
        x = _merge(x, n_mix[l], outs, w_gate, w_br[l].astype(BF16), w_o[l].astype(BF16), tm)
        x = _ffn(x, n_ffn2[l], ffn2_in[l].astype(BF16), ffn2_out[l].astype(BF16), tm, 512)
        p_all = jnp.concatenate([p_prompt[l].reshape(n_p, -1), p_sample[l].reshape(nb, -1)], axis=0)
        x = _ple(x, n_ple[l], p_all, ple_wg[l].astype(BF16), ple_wp[l].astype(BF16), norm_final,
                 l == depth - 1, tm)

        def kv5(a, rows):
            return a.reshape(rows, -1, 2, 1, HEAD_DIM)

        win_p = kv5(kv_win[:n_p], batch)
        st_p.append((kv5(kv_cmp[:n_p], batch), kv5(kv_sel[:n_p], batch), win_p[:, seq - min(NSA_WINDOW, seq):],
                     lat[:n_p, :MLA_DC + MLA_DR].reshape(batch, seq, -1), kv5(fox_kv[:n_p], batch),
                     logf[:n_p, :N_HEADS].reshape(batch, seq, N_HEADS)))
        win_s = jnp.concatenate([state_nsa_win[l], kv5(kv_win[n_p:], nb)], axis=1)[:, 1:]
        st_s.append((kv5(kv_cmp[n_p:], nb), kv5(kv_sel[n_p:], nb), win_s,
                     lat[n_p:, :MLA_DC + MLA_DR].reshape(nb, 1, -1), kv5(fox_kv[n_p:], nb),
                     logf[n_p:, :N_HEADS].reshape(nb, 1, N_HEADS), gm_v[n_p:].reshape(nb, 1, N_HEADS, HEAD_DIM)))

    y = x
    outs_p = [jnp.stack(a) for a in zip(*st_p)]
    outs_s = [jnp.stack(a) for a in zip(*st_s)]
    return (y[:n_p].reshape(batch, seq, d), y[n_p:].reshape(nb, 1, d), *outs_p, *outs_s)
```

```python
import functools

import numpy as np
import jax
import jax.numpy as jnp
from jax import lax
from jax.experimental import pallas as pl
from jax.experimental.pallas import tpu as pltpu

D_MODEL = 2048
N_HEADS = 8
HEAD_DIM = 64
MIX_W = N_HEADS * HEAD_DIM
N_BRANCH = 4
D_FF = 2 * D_MODEL
PLE_DIM = 256
ROPE_THETA = 500000.0
ROT_DIM = HEAD_DIM // 4
NSA_CMP = 32
NSA_SEL = 64
NSA_TOPK = 16
NSA_WINDOW = 512
GM_CHUNK = 128
MLA_DQ = D_MODEL // 4
MLA_DC = D_MODEL // 16
MLA_DR = HEAD_DIM // 2
MLA_THETA = 10000.0
EPS = 1e-6
NEG = -1e30
FORCE = 1e4
ATT_SCALE = HEAD_DIM ** -0.5
MLA_SCALE = (HEAD_DIM + MLA_DR) ** -0.5
PAGE_SIZE = 128

F32 = jnp.float32
BF16 = jnp.bfloat16
LANES = 128
MLA_QW = 2 * LANES
VMEM_LIMIT = 56 * 1024 * 1024

Z_NSA_Q, Z_NSA_QS, Z_GM_U, Z_GM_V, Z_MLA_CQ, Z_FOX_Q = (i * MIX_W for i in range(6))
Z_NSA_CMP = 6 * MIX_W
Z_NSA_SEL = Z_NSA_CMP + LANES
Z_NSA_WIN = Z_NSA_SEL + LANES
Z_NSA_KSS = Z_NSA_WIN + LANES
Z_NSA_KSW = Z_NSA_KSS + LANES
Z_NSA_G = Z_NSA_KSW + LANES
Z_MLA_CKV = Z_NSA_G + LANES
Z_MLA_KR = Z_MLA_CKV + LANES
Z_MLA_KRS = Z_MLA_KR + LANES
Z_FOX_KV = Z_MLA_KRS + LANES
Z_FOX_F = Z_FOX_KV + LANES
Z_W = Z_FOX_F + LANES


def _cparams(sem):
    return pltpu.CompilerParams(dimension_semantics=sem, vmem_limit_bytes=VMEM_LIMIT)


def _const_spec(shape):
    nd = len(shape)
    return pl.BlockSpec(shape, lambda *_: (0,) * nd, pipeline_mode=pl.Buffered(1))


def _rms(x, g):
    return x * lax.rsqrt(jnp.mean(x * x, axis=-1, keepdims=True) + EPS) * g


def _dot(a, b):
    return jnp.dot(a, b, preferred_element_type=F32)


def _dot_nt(a, b):
    return lax.dot_general(a, b, (((1,), (1,)), ((), ())), preferred_element_type=F32)


def _ffn_kernel(x_ref, g_ref, wg_ref, wu_ref, wo_ref, o_ref, h_scr, acc_scr):
    k = pl.program_id(1)

    @pl.when(k == 0)
    def _():
        h_scr[...] = _rms(x_ref[...], g_ref[...]).astype(BF16)
        acc_scr[...] = jnp.zeros_like(acc_scr)

    h = h_scr[...]
    g = _dot(h, wg_ref[...])
    u = _dot(h, wu_ref[...])
    a = (g * jax.nn.sigmoid(g)) * u
    acc_scr[...] += _dot(a.astype(BF16), wo_ref[...])

    @pl.when(k == pl.num_programs(1) - 1)
    def _():
        o_ref[...] = x_ref[...] + 0.5 * acc_scr[...]


def _ffn(x, g, w_in, w_out, tm, tc):
    n, d = x.shape
    dff = w_out.shape[0]
    nc = dff // tc
    return pl.pallas_call(
        _ffn_kernel,
        out_shape=jax.ShapeDtypeStruct((n, d), F32),
        grid=(n // tm, nc),
        in_specs=[
            pl.BlockSpec((tm, d), lambda i, k: (i, 0)),
            pl.BlockSpec((1, d), lambda i, k: (0, 0)),
            pl.BlockSpec((d, tc), lambda i, k: (0, k)),
            pl.BlockSpec((d, tc), lambda i, k: (0, k + nc)),
            pl.BlockSpec((tc, d), lambda i, k: (k, 0)),
        ],
        out_specs=pl.BlockSpec((tm, d), lambda i, k: (i, 0)),
        scratch_shapes=[pltpu.VMEM((tm, d), BF16), pltpu.VMEM((tm, d), F32)],
        compiler_params=_cparams(("parallel", "arbitrary")),
        name="ffn",
    )(x, g.reshape(1, d), w_in, w_in, w_out)


def _proj_kernel(x_ref, g_ref, w_ref, o_ref):
    h = _rms(x_ref[...], g_ref[...]).astype(BF16)
    o_ref[...] = _dot(h, w_ref[...])


def _proj(x, g, w, tm):
    n, d = x.shape
    zw = w.shape[1]
    return pl.pallas_call(
        _proj_kernel,
        out_shape=jax.ShapeDtypeStruct((n, zw), F32),
        grid=(n // tm,),
        in_specs=[
            pl.BlockSpec((tm, d), lambda i: (i, 0)),
            pl.BlockSpec((1, d), lambda i: (0, 0)),
            _const_spec((d, zw)),
        ],
        out_specs=pl.BlockSpec((tm, zw), lambda i: (i, 0)),
        compiler_params=_cparams(("parallel",)),
        name="in_proj",
    )(x, g.reshape(1, d), w)


def _merge_kernel(x_ref, g_ref, o0_ref, o1_ref, o2_ref, o3_ref, wg_ref, wbr_ref, wo_ref,
                  out_ref, h_scr, acc_scr):
    s = pl.program_id(1)
    half = s % 2

    @pl.when(s == 0)
    def _():
        h_scr[...] = _rms(x_ref[...], g_ref[...]).astype(BF16)
        acc_scr[...] = jnp.zeros_like(acc_scr)

    gate = jax.nn.sigmoid(_dot(h_scr[...], wg_ref[...]))
    for b, o_ref in enumerate((o0_ref, o1_ref, o2_ref, o3_ref)):
        @pl.when(s // 2 == b)
        def _(o_ref=o_ref):
            acc_scr[half] += gate * _dot(o_ref[...], wbr_ref[0])

    @pl.when(s == pl.num_programs(1) - 1)
    def _():
        hd = wo_ref.shape[0] // 2
        y = _dot(acc_scr[0].astype(BF16), wo_ref[:hd, :])
        y += _dot(acc_scr[1].astype(BF16), wo_ref[hd:, :])
        out_ref[...] = x_ref[...] + y


def _merge(x, g, outs, w_gate, w_br, w_o, tm):
    n, d = x.shape
    hd = d // 2
    o_spec = pl.BlockSpec((tm, MIX_W), lambda i, s: (i, 0))
    return pl.pallas_call(
        _merge_kernel,
        out_shape=jax.ShapeDtypeStruct((n, d), F32),
        grid=(n // tm, 2 * N_BRANCH),
        in_specs=[
            pl.BlockSpec((tm, d), lambda i, s: (i, 0)),
            pl.BlockSpec((1, d), lambda i, s: (0, 0)),
            o_spec, o_spec, o_spec, o_spec,
            pl.BlockSpec((d, hd), lambda i, s: (0, s)),
            pl.BlockSpec((1, MIX_W, hd), lambda i, s: (s // 2, 0, s % 2)),
            _const_spec((d, d)),
        ],
        out_specs=pl.BlockSpec((tm, d), lambda i, s: (i, 0)),
        scratch_shapes=[pltpu.VMEM((tm, d), BF16), pltpu.VMEM((2, tm, hd), F32)],
        compiler_params=_cparams(("parallel", "arbitrary")),
        name="merge",
    )(x, g.reshape(1, d), *outs, w_gate, w_br, w_o)


def _ple_kernel(x_ref, g_ref, p_ref, wg_ref, wp_ref, gf_ref, o_ref, *, final):
    x = x_ref[...]
    gate = jax.nn.sigmoid(_dot(_rms(x, g_ref[...]).astype(BF16), wg_ref[...]))
    y = x + gate * _dot(p_ref[...].astype(BF16), wp_ref[...])
    if final:
        y = _rms(y, gf_ref[...])
    o_ref[...] = y


def _ple(x, g, p, w_g, w_p, g_final, final, tm):
    n, d = x.shape
    pd = p.shape[1]
    return pl.pallas_call(
        functools.partial(_ple_kernel, final=final),
        out_shape=jax.ShapeDtypeStruct((n, d), F32),
        grid=(n // tm,),
        in_specs=[
            pl.BlockSpec((tm, d), lambda i: (i, 0)),
            pl.BlockSpec((1, d), lambda i: (0, 0)),
            pl.BlockSpec((tm, pd), lambda i: (i, 0)),
            _const_spec((d, d)),
            _const_spec((pd, d)),
            pl.BlockSpec((1, d), lambda i: (0, 0)),
        ],
        out_specs=pl.BlockSpec((tm, d), lambda i: (i, 0)),
        compiler_params=_cparams(("parallel",)),
        name="ple",
    )(x, g.reshape(1, d), p, w_g, w_p, g_final.reshape(1, d))


def _gelu(x):
    return 0.5 * x * (1.0 + lax.erf(x * np.float32(np.sqrt(0.5))))


def _log_sigmoid(x):
    return jnp.minimum(x, 0.0) - jnp.log1p(jnp.exp(-jnp.abs(x)))


def _prep_kernel(zq_ref, zqs_ref, zu_ref, zv_ref, zcq_ref, zsel_ref, zwin_ref, zkss_ref, zksw_ref,
                 zg_ref, zckv_ref, zkr_ref, zkrs_ref, zf_ref, tn_ref, tm_ref,
                 gmn_ref, qn_ref, kvn_ref, bf_ref, wqn_ref, wbd_ref, wa_ref, wb_ref,
                 oq_ref, oqr_ref, osel_ref, owin_ref, ogate_ref, ou_ref, ov_ref, oqc_ref,
                 olat_ref, ologf_ref):
    tn = tn_ref[...]
    cos_q = jnp.concatenate([tn[:, 0:LANES]] * (MIX_W // LANES), axis=1)
    sin_q = jnp.concatenate([tn[:, LANES:2 * LANES]] * (MIX_W // LANES), axis=1)
    cos_k = tn[:, 2 * LANES:3 * LANES]
    sin_k = tn[:, 3 * LANES:4 * LANES]
    q = zq_ref[...]
    oq_ref[...] = (q * ATT_SCALE).astype(BF16)
    oqr_ref[...] = ((q * cos_q + zqs_ref[...] * sin_q) * ATT_SCALE).astype(BF16)
    osel_ref[...] = zsel_ref[...] * cos_k + zkss_ref[...] * sin_k
    owin_ref[...] = zwin_ref[...] * cos_k + zksw_ref[...] * sin_k
    ogate_ref[...] = jax.nn.sigmoid(zg_ref[...])

    ou_ref[...] = _gelu(zu_ref[...])
    gv = _gelu(zv_ref[...])
    gc = gv - jnp.mean(gv, axis=-1, keepdims=True)
    ov_ref[...] = gc * lax.rsqrt(jnp.mean(gc * gc, axis=-1, keepdims=True) + EPS) * gmn_ref[...]

    tm = tm_ref[...]
    cqn = _rms(zcq_ref[...], qn_ref[...]).astype(BF16)
    q_nope = _dot(cqn, wqn_ref[...]).astype(BF16)
    qa = _dot(q_nope, wbd_ref[...]) + _dot(cqn, wa_ref[...])
    qb = _dot(cqn, wb_ref[...])
    cos_m = jnp.concatenate([tm[:, :MLA_QW]] * N_HEADS, axis=1)
    sin_m = jnp.concatenate([tm[:, MLA_QW:]] * N_HEADS, axis=1)
    oqc_ref[...] = ((qa * cos_m + qb * sin_m) * MLA_SCALE).astype(BF16)
    c = _rms(zckv_ref[...], kvn_ref[...])
    kr = zkr_ref[...] * tm[:, LANES:2 * LANES] + zkrs_ref[...] * tm[:, MLA_QW + LANES:]
    olat_ref[...] = jnp.concatenate([c, kr], axis=1)
    ologf_ref[...] = _log_sigmoid(zf_ref[...] + bf_ref[...])


def _prep(z, tab_nsa, tab_mla, gm_norm, q_norm, kv_norm, fox_bf, wqn, wbd, wa, wb, tm):
    n = z.shape[0]

    def zb(off, w):
        return pl.BlockSpec((tm, w), lambda i, o=off // w: (i, o))

    def row(w):
        return pl.BlockSpec((tm, w), lambda i: (i, 0))

    def vec(w):
        return pl.BlockSpec((1, w), lambda i: (0, 0))

    z_specs = [zb(Z_NSA_Q, MIX_W), zb(Z_NSA_QS, MIX_W), zb(Z_GM_U, MIX_W), zb(Z_GM_V, MIX_W),
               zb(Z_MLA_CQ, MIX_W), zb(Z_NSA_SEL, LANES), zb(Z_NSA_WIN, LANES), zb(Z_NSA_KSS, LANES),
               zb(Z_NSA_KSW, LANES), zb(Z_NSA_G, LANES), zb(Z_MLA_CKV, LANES), zb(Z_MLA_KR, LANES),
               zb(Z_MLA_KRS, LANES), zb(Z_FOX_F, LANES)]
    qcw = N_HEADS * MLA_QW
    out_shape = (
        jax.ShapeDtypeStruct((n, MIX_W), BF16), jax.ShapeDtypeStruct((n, MIX_W), BF16),
        jax.ShapeDtypeStruct((n, LANES), F32), jax.ShapeDtypeStruct((n, LANES), F32),
        jax.ShapeDtypeStruct((n, LANES), F32),
        jax.ShapeDtypeStruct((n, MIX_W), F32), jax.ShapeDtypeStruct((n, MIX_W), F32),
        jax.ShapeDtypeStruct((n, qcw), BF16), jax.ShapeDtypeStruct((n, MLA_QW), F32),
        jax.ShapeDtypeStruct((n, LANES), F32),
    )
    out_specs = (row(MIX_W), row(MIX_W), row(LANES), row(LANES), row(LANES), row(MIX_W), row(MIX_W),
                 row(qcw), row(MLA_QW), row(LANES))
    return pl.pallas_call(
        _prep_kernel,
        out_shape=out_shape,
        grid=(n // tm,),
        in_specs=z_specs + [row(4 * LANES), row(2 * MLA_QW), vec(MIX_W), vec(MLA_DQ), vec(LANES), vec(LANES),
                            _const_spec(wqn.shape), _const_spec(wbd.shape), _const_spec(wa.shape),
                            _const_spec(wb.shape)],
        out_specs=out_specs,
        compiler_params=_cparams(("parallel",)),
        name="mixer_prep",
    )(*([z] * len(z_specs)), tab_nsa, tab_mla, gm_norm.reshape(1, -1), q_norm.reshape(1, -1),
      kv_norm.reshape(1, -1), fox_bf, wqn, wbd, wa, wb)


def _stack_heads(x, width):
    return jnp.concatenate([x[:, h * width:(h + 1) * width] for h in range(N_HEADS)], axis=0)


def _unstack_heads(x, t):
    return jnp.concatenate([x[h * t:(h + 1) * t] for h in range(N_HEADS)], axis=1)


def _tile_rows(x):
    return jnp.concatenate([x] * N_HEADS, axis=0)


def _online_update(s, v, m, l, acc):
    m_new = jnp.maximum(m, jnp.max(s, axis=-1, keepdims=True))
    alpha = jnp.exp(m - m_new)
    p = jnp.exp(s - m_new)
    l = alpha * l + jnp.sum(p, axis=-1, keepdims=True)
    acc = alpha * acc + _dot(p.astype(BF16), v)
    return m_new, l, acc


def _flash_init(rows, width):
    return (jnp.full((rows, 1), NEG, F32), jnp.zeros((rows, 1), F32), jnp.zeros((rows, width), F32))


def _gmlp_kernel(u_ref, v_ref, ws_ref, bs_ref, o_ref):
    c = GM_CHUNK
    tril = (lax.broadcasted_iota(jnp.int32, (c, c), 0) >= lax.broadcasted_iota(jnp.int32, (c, c), 1)).astype(F32)
    grp = lax.broadcasted_iota(jnp.int32, (c, MIX_W), 1) // HEAD_DIM
    ws = [(ws_ref[g] * tril).astype(BF16) for g in range(N_HEADS)]
    for ch in range(u_ref.shape[0] // c):
        v = v_ref[ch * c:(ch + 1) * c, :].astype(BF16)
        z = bs_ref[...]
        for g in range(N_HEADS):
            z = z + jnp.where(grp == g, _dot(ws[g], v), 0.0)
        o_ref[ch * c:(ch + 1) * c, :] = (u_ref[ch * c:(ch + 1) * c, :] * z).astype(o_ref.dtype)


def _gmlp_prompt(u, v, ws, bs_exp, n_rows, tq):
    return pl.pallas_call(
        _gmlp_kernel,
        out_shape=jax.ShapeDtypeStruct((n_rows, MIX_W), BF16),
        grid=(n_rows // tq,),
        in_specs=[
            pl.BlockSpec((tq, MIX_W), lambda i: (i, 0)),
            pl.BlockSpec((tq, MIX_W), lambda i: (i, 0)),
            _const_spec(ws.shape),
            _const_spec(bs_exp.shape),
        ],
        out_specs=pl.BlockSpec((tq, MIX_W), lambda i: (i, 0)),
        compiler_params=_cparams(("parallel",)),
        name="gmlp_prompt",
    )(u, v, ws, bs_exp)


def _split3(x):
    hi = x.astype(BF16)
    r = x - hi.astype(F32)
    mid = r.astype(BF16)
    lo = (r - mid.astype(F32)).astype(BF16)
    return hi, mid, lo


def _cumsum_kernel(x_ref, col_ref, row_ref, *, blk):
    tri = (lax.broadcasted_iota(jnp.int32, (blk, blk), 0)
           >= lax.broadcasted_iota(jnp.int32, (blk, blk), 1)).astype(BF16)
    carry = jnp.zeros((1, x_ref.shape[1]), F32)
    for c in range(x_ref.shape[0] // blk):
        hi, mid, lo = _split3(x_ref[c * blk:(c + 1) * blk, :])
        cs = (_dot(tri, hi) + _dot(tri, mid)) + _dot(tri, lo) + carry
        col_ref[c * blk:(c + 1) * blk, :] = cs
        row_ref[:, c * blk:(c + 1) * blk] = cs.T[:row_ref.shape[0], :]
        carry = cs[blk - 1:blk, :]


def _fox_cumsum(logf, batch, seq):
    return pl.pallas_call(
        functools.partial(_cumsum_kernel, blk=256),
        out_shape=(jax.ShapeDtypeStruct((batch * seq, LANES), F32),
                   jax.ShapeDtypeStruct((batch * N_HEADS, seq), F32)),
        grid=(batch,),
        in_specs=[pl.BlockSpec((seq, LANES), lambda b: (b, 0))],
        out_specs=(pl.BlockSpec((seq, LANES), lambda b: (b, 0)),
                   pl.BlockSpec((N_HEADS, seq), lambda b: (b, 0))),
        compiler_params=_cparams(("parallel",)),
        name="fox_cumsum",
    )(logf)


def _fox_kernel(q_ref, kv_ref, cc_ref, cr_ref, o_ref, *, tq, tk):
    i = pl.program_id(1)
    q0 = i * tq
    qs = _stack_heads((q_ref[...] * ATT_SCALE).astype(BF16), HEAD_DIM)
    cc = cc_ref[...]
    cq = jnp.concatenate([cc[:, h:h + 1] for h in range(N_HEADS)], axis=0)
    t_pos = q0 + lax.broadcasted_iota(jnp.int32, (tq, tk), 0)
    l_off = lax.broadcasted_iota(jnp.int32, (tq, tk), 1)

    def body(j, carry):
        k0 = pl.multiple_of(j * tk, tk)
        kv = kv_ref[pl.ds(k0, tk), :]
        s = _dot_nt(qs, kv[:, :HEAD_DIM].astype(BF16))
        ck = cr_ref[:, pl.ds(k0, tk)]
        ok = (k0 + l_off) <= t_pos
        s = jnp.concatenate(
            [jnp.where(ok, s[h * tq:(h + 1) * tq] - ck[h:h + 1, :], NEG) for h in range(N_HEADS)], axis=0)
        return _online_update(s + cq, kv[:, HEAD_DIM:].astype(BF16), *carry)

    nkb = (q0 + tq - 1) // tk + 1
    m, l, acc = lax.fori_loop(0, nkb, body, _flash_init(N_HEADS * tq, HEAD_DIM))
    o_ref[...] = _unstack_heads(acc / l, tq).astype(o_ref.dtype)


def _fox_prompt(z, c_col, c_row, batch, seq, tq, tk):
    nq = seq // tq
    return pl.pallas_call(
        functools.partial(_fox_kernel, tq=tq, tk=tk),
        out_shape=jax.ShapeDtypeStruct((batch * seq, MIX_W), BF16),
        grid=(batch, nq),
        in_specs=[
            pl.BlockSpec((tq, MIX_W), lambda b, i: (b * nq + i, Z_FOX_Q // MIX_W)),
            pl.BlockSpec((seq, LANES), lambda b, i: (b, Z_FOX_KV // LANES)),
            pl.BlockSpec((tq, LANES), lambda b, i: (b * nq + i, 0)),
            pl.BlockSpec((N_HEADS, seq), lambda b, i: (b, 0)),
        ],
        out_specs=pl.BlockSpec((tq, MIX_W), lambda b, i: (b * nq + i, 0)),
        compiler_params=_cparams(("parallel", "parallel")),
        name="fox_prompt",
    )(z, z, c_col, c_row)


def _mla_kernel(q_ref, lat_ref, wuv_ref, o_ref, *, tq, tk):
    i = pl.program_id(1)
    q0 = i * tq
    qs = _stack_heads(q_ref[...], MLA_QW)
    t_pos = q0 + lax.broadcasted_iota(jnp.int32, (tq, tk), 0)
    l_off = lax.broadcasted_iota(jnp.int32, (tq, tk), 1)

    def body(j, carry):
        k0 = pl.multiple_of(j * tk, tk)
        lat = lat_ref[pl.ds(k0, tk), :].astype(BF16)
        s = _dot_nt(qs, lat)
        ok = _tile_rows((k0 + l_off) <= t_pos)
        return _online_update(jnp.where(ok, s, NEG), lat[:, :MLA_DC], *carry)

    nkb = (q0 + tq - 1) // tk + 1
    m, l, acc = lax.fori_loop(0, nkb, body, _flash_init(N_HEADS * tq, MLA_DC))
    o_lat = _unstack_heads(acc / l, tq).astype(BF16)
    o_ref[...] = _dot(o_lat, wuv_ref[...]).astype(o_ref.dtype)


def _mla_prompt(qc, lat, wuv_bd, batch, seq, tq, tk):
    nq = seq // tq
    return pl.pallas_call(
        functools.partial(_mla_kernel, tq=tq, tk=tk),
        out_shape=jax.ShapeDtypeStruct((batch * seq, MIX_W), BF16),
        grid=(batch, nq),
        in_specs=[
            pl.BlockSpec((tq, N_HEADS * MLA_QW), lambda b, i: (b * nq + i, 0)),
            pl.BlockSpec((seq, MLA_QW), lambda b, i: (b, 0)),
            _const_spec(wuv_bd.shape),
        ],
        out_specs=pl.BlockSpec((tq, MIX_W), lambda b, i: (b * nq + i, 0)),
        compiler_params=_cparams(("parallel", "parallel")),
        name="mla_prompt",
    )(qc, lat, wuv_bd)


def _cmp_kernel(x_ref, w_ref, e_ref, o_ref):
    half = NSA_CMP * LANES
    for part, out in ((0, e_ref), (1, o_ref)):
        acc = jnp.zeros(out.shape, F32)
        for j in range(NSA_CMP):
            lo = part * half + j * LANES
            acc = acc + x_ref[:, lo:lo + LANES] * w_ref[j:j + 1, :]
        out[...] = acc * (1.0 / NSA_CMP)


def _nsa_compress(kv_rows, w, n_rows, tr):
    return pl.pallas_call(
        _cmp_kernel,
        out_shape=(jax.ShapeDtypeStruct((n_rows, LANES), F32), jax.ShapeDtypeStruct((n_rows, LANES), F32)),
        grid=(n_rows // tr,),
        in_specs=[pl.BlockSpec((tr, NSA_SEL * LANES), lambda i: (i, 0)), _const_spec(w.shape)],
        out_specs=(pl.BlockSpec((tr, LANES), lambda i: (i, 0)), pl.BlockSpec((tr, LANES), lambda i: (i, 0))),
        compiler_params=_cparams(("parallel",)),
        name="nsa_compress",
    )(kv_rows, w)


def _topk_mask(v, k):
    n = v.shape[1]
    idx = lax.broadcasted_iota(jnp.int32, v.shape, 1)
    rank = jnp.zeros(v.shape, F32)
    for m in range(n):
        col = v[:, m:m + 1]
        rank = rank + jnp.where(idx > m, jnp.where(col >= v, 1.0, 0.0), jnp.where(col > v, 1.0, 0.0))
    return jnp.where(rank < k, 1.0, 0.0)


def _nsa_kernel(q_ref, qr_ref, g_ref, ce_ref, co_ref, sel_ref, win_ref, o_ref, selx_scr, *, tq, tk):
    i = pl.program_id(1)
    q0 = i * tq
    seq = sel_ref.shape[0]
    ns = seq // NSA_SEL
    rows = N_HEADS * tq
    qs = _stack_heads(q_ref[...], HEAD_DIM)
    qrs = _stack_heads(qr_ref[...], HEAD_DIM)
    pos1 = q0 + lax.broadcasted_iota(jnp.int32, (tq, 1), 0)
    pos = _tile_rows(pos1)

    ce = ce_ref[...]
    co = co_ref[...]
    kc = jnp.concatenate([ce[:, :HEAD_DIM], co[:, :HEAD_DIM]], axis=0).astype(BF16)
    vc = jnp.concatenate([ce[:, HEAD_DIM:], co[:, HEAD_DIM:]], axis=0).astype(BF16)
    s = _dot_nt(qs, kc)
    n_idx = lax.broadcasted_iota(jnp.int32, (1, 2 * ns), 1)
    end_pos = jnp.where(n_idx < ns, n_idx * NSA_SEL + NSA_CMP - 1, (n_idx - ns) * NSA_SEL + NSA_SEL - 1)
    ok = end_pos <= pos
    s = jnp.where(ok, s, NEG)
    e = jnp.exp(s - jnp.max(s, axis=-1, keepdims=True))
    p = e / jnp.sum(e, axis=-1, keepdims=True) * jnp.where(ok, 1.0, 0.0)
    o_cmp = _dot(p.astype(BF16), vc)

    ph = p[0:tq]
    for h in range(1, N_HEADS):
        ph = ph + p[h * tq:(h + 1) * tq]
    imp = ph[:, :ns] + ph[:, ns:]
    blk = lax.broadcasted_iota(jnp.int32, (tq, ns), 1)
    cur = pos1 // NSA_SEL
    imp = jnp.where(blk > cur, NEG, imp)
    for forced in (0, cur, cur - 1):
        imp = jnp.where(blk == forced, FORCE, imp)
    sel = _topk_mask(imp, NSA_TOPK).astype(BF16)
    expand = (lax.broadcasted_iota(jnp.int32, (ns, seq), 1) // NSA_SEL
              == lax.broadcasted_iota(jnp.int32, (ns, seq), 0)).astype(BF16)
    selx_scr[...] = _dot(sel, expand)

    t_pos = q0 + lax.broadcasted_iota(jnp.int32, (tq, tk), 0)
    l_off = lax.broadcasted_iota(jnp.int32, (tq, tk), 1)

    def sel_body(j, carry):
        k0 = pl.multiple_of(j * tk, tk)
        kv = sel_ref[pl.ds(k0, tk), :]
        s = _dot_nt(qrs, kv[:, :HEAD_DIM].astype(BF16))
        ok = _tile_rows(((k0 + l_off) <= t_pos) & (selx_scr[:, pl.ds(k0, tk)] > 0.5))
        return _online_update(jnp.where(ok, s, NEG), kv[:, HEAD_DIM:].astype(BF16), *carry)

    nkb = (q0 + tq - 1) // tk + 1
    _, l_sel, a_sel = lax.fori_loop(0, nkb, sel_body, _flash_init(rows, HEAD_DIM))

    def win_body(j, carry):
        k0 = pl.multiple_of(j * tk, tk)
        kv = win_ref[pl.ds(k0, tk), :]
        s = _dot_nt(qrs, kv[:, :HEAD_DIM].astype(BF16))
        dist = t_pos - (k0 + l_off)
        ok = _tile_rows((dist >= 0) & (dist <= NSA_WINDOW))
        return _online_update(jnp.where(ok, s, NEG), kv[:, HEAD_DIM:].astype(BF16), *carry)

    first = jnp.maximum(q0 - NSA_WINDOW, 0) // tk
    _, l_win, a_win = lax.fori_loop(first, nkb, win_body, _flash_init(rows, HEAD_DIM))

    g = g_ref[...]
    gate = [jnp.concatenate([g[:, 3 * h + c:3 * h + c + 1] for h in range(N_HEADS)], axis=0) for c in range(3)]
    o = gate[0] * o_cmp + gate[1] * (a_sel / l_sel) + gate[2] * (a_win / l_win)
    o_ref[...] = _unstack_heads(o, tq).astype(o_ref.dtype)


def _nsa_prompt(q, qr, gates, cmp_e, cmp_o, kv_sel, kv_win, batch, seq, tq, tk):
    nq = seq // tq
    ns = seq // NSA_SEL

    def qspec(w):
        return pl.BlockSpec((tq, w), lambda b, i: (b * nq + i, 0))

    def bspec(r, w):
        return pl.BlockSpec((r, w), lambda b, i: (b, 0))

    return pl.pallas_call(
        functools.partial(_nsa_kernel, tq=tq, tk=tk),
        out_shape=jax.ShapeDtypeStruct((batch * seq, MIX_W), BF16),
        grid=(batch, nq),
        in_specs=[qspec(MIX_W), qspec(MIX_W), qspec(LANES), bspec(ns, LANES), bspec(ns, LANES),
                  bspec(seq, LANES), bspec(seq, LANES)],
        out_specs=qspec(MIX_W),
        scratch_shapes=[pltpu.VMEM((tq, seq), F32)],
        compiler_params=_cparams(("parallel", "parallel")),
        name="nsa_prompt",
    )(q, qr, gates, cmp_e, cmp_o, kv_sel, kv_win)


def _page_dma(cache_ref, pt_ref, b, base, dst_fn, sem, n_pages, start):
    def body(j, c):
        cp = pltpu.make_async_copy(cache_ref.at[pt_ref[b, j] + base], dst_fn(j), sem)
        if start:
            cp.start()
        else:
            cp.wait()
        return c

    lax.fori_loop(0, n_pages, body, 0)


def _prefetch_schedule(step, n_steps, issue):
    slot = step % 2

    @pl.when(step == 0)
    def _():
        issue(step, slot, True)

    @pl.when(step + 1 < n_steps)
    def _():
        issue(step + 1, 1 - slot, True)

    issue(step, slot, False)
    return slot


def _softmax_parts(parts):
    m = parts[0].max(axis=-1, keepdims=True)
    for s in parts[1:]:
        m = jnp.maximum(m, s.max(axis=-1, keepdims=True))
    es = [jnp.exp(s - m) for s in parts]
    l = es[0].sum(axis=-1, keepdims=True)
    for e in es[1:]:
        l = l + e.sum(axis=-1, keepdims=True)
    return es, l


def _rowdot(q, k_row):
    return jnp.sum(q.astype(F32) * k_row.astype(BF16).astype(F32), axis=-1, keepdims=True)


def _nsa_s1_kernel(pt_ref, cache_ref, q_ref, new_ref, w_ref, imp_ref, o_ref, buf, sem, *, base, past):
    b = pl.program_id(0)
    n_pages = past // PAGE_SIZE
    ns = past // NSA_SEL

    def issue(step, slot, start):
        _page_dma(cache_ref, pt_ref, step, base,
                  lambda j: buf.at[slot, pl.ds(pl.multiple_of(j * PAGE_SIZE, PAGE_SIZE), PAGE_SIZE)],
                  sem.at[slot], n_pages, start)

    slot = _prefetch_schedule(b, pl.num_programs(0), issue)

    acc_e = jnp.zeros((ns, LANES), F32)
    acc_o = jnp.zeros((ns, LANES), F32)
    for j in range(NSA_CMP):
        wj = w_ref[j:j + 1, :]
        acc_e = acc_e + buf[slot, pl.ds(j, ns, stride=NSA_SEL), :] * wj
        acc_o = acc_o + buf[slot, pl.ds(NSA_CMP + j, ns, stride=NSA_SEL), :] * wj
    ce = (acc_e * (1.0 / NSA_CMP)).astype(BF16)
    co = (acc_o * (1.0 / NSA_CMP)).astype(BF16)
    te = new_ref[0] * w_ref[0:1, :] * (1.0 / NSA_CMP)
    to = jnp.zeros_like(te)

    q = q_ref[0]
    blk = lax.broadcasted_iota(jnp.int32, (1, ns), 1)
    ok_e = blk * NSA_SEL + NSA_CMP - 1 <= past
    ok_o = blk * NSA_SEL + NSA_SEL - 1 <= past
    ok_te = ns * NSA_SEL + NSA_CMP - 1 <= past
    ok_to = ns * NSA_SEL + NSA_SEL - 1 <= past
    s_e = jnp.where(ok_e, _dot_nt(q, ce), NEG)
    s_o = jnp.where(ok_o, _dot_nt(q, co), NEG)
    s_te = jnp.where(ok_te, _rowdot(q, te), NEG)
    s_to = jnp.where(ok_to, _rowdot(q, to), NEG)
    (e_e, e_o, e_te, e_to), l = _softmax_parts([s_e, s_o, s_te, s_to])
    p_e = e_e / l * jnp.where(ok_e, 1.0, 0.0)
    p_o = e_o / l * jnp.where(ok_o, 1.0, 0.0)
    p_te = e_te / l * (1.0 if ok_te else 0.0)
    p_to = e_to / l * (1.0 if ok_to else 0.0)
    o = _dot(p_e.astype(BF16), ce) + _dot(p_o.astype(BF16), co) + p_te * te + p_to * to
    o_ref[0] = o[:, HEAD_DIM:]
    imp_past = jnp.sum(p_e + p_o, axis=0, keepdims=True)
    imp_tail = jnp.sum(p_te + p_to, axis=0, keepdims=True)
    imp_ref[0] = jnp.concatenate([imp_past, jnp.broadcast_to(imp_tail, (1, LANES))], axis=1)


def _nsa_sample_cmp(page_table, cache, q8, kv_new, w, base, past):
    nb = q8.shape[0]
    ns = past // NSA_SEL
    grid_spec = pltpu.PrefetchScalarGridSpec(
        num_scalar_prefetch=1,
        grid=(nb,),
        in_specs=[
            pl.BlockSpec(memory_space=pl.ANY),
            pl.BlockSpec((1, N_HEADS, LANES), lambda b, pt: (b, 0, 0)),
            pl.BlockSpec((1, 1, LANES), lambda b, pt: (b, 0, 0)),
            pl.BlockSpec(w.shape, lambda b, pt: (0, 0)),
        ],
        out_specs=(pl.BlockSpec((1, 1, ns + LANES), lambda b, pt: (b, 0, 0)),
                   pl.BlockSpec((1, N_HEADS, HEAD_DIM), lambda b, pt: (b, 0, 0))),
        scratch_shapes=[pltpu.VMEM((2, past, LANES), F32), pltpu.SemaphoreType.DMA((2,))],
    )
    return pl.pallas_call(
        functools.partial(_nsa_s1_kernel, base=base, past=past),
        out_shape=(jax.ShapeDtypeStruct((nb, 1, ns + LANES), F32),
                   jax.ShapeDtypeStruct((nb, N_HEADS, HEAD_DIM), F32)),
        grid_spec=grid_spec,
        compiler_params=_cparams(("arbitrary",)),
        name="nsa_sample_cmp",
    )(page_table, cache, q8, kv_new, w)


def _topk_kernel(imp_ref, idx_ref, *, cur, k):
    x = imp_ref[...]
    lane = lax.broadcasted_iota(jnp.int32, x.shape, 1)
    x = jnp.where((lane == 0) | (lane == cur) | (lane == cur - 1), FORCE, jnp.where(lane > cur, NEG, x))
    lane_f = lane.astype(F32)
    out_lane = lax.broadcasted_iota(jnp.int32, idx_ref.shape, 1)
    out = jnp.zeros(idx_ref.shape, F32)
    for i in range(k):
        m = jnp.max(x, axis=-1, keepdims=True)
        pick = jnp.min(jnp.where(x == m, lane_f, 1e9), axis=-1, keepdims=True)
        out = jnp.where(out_lane == i, pick, out)
        x = jnp.where(lane_f == pick, -3.0e38, x)
    idx_ref[...] = out.astype(jnp.int32)


def _nsa_sample_topk(imp, cur):
    nb, w = imp.shape
    return pl.pallas_call(
        functools.partial(_topk_kernel, cur=cur, k=min(NSA_TOPK, cur + 1)),
        out_shape=jax.ShapeDtypeStruct((nb, LANES), jnp.int32),
        in_specs=[pl.BlockSpec((nb, w), lambda: (0, 0))],
        out_specs=pl.BlockSpec((nb, LANES), lambda: (0, 0)),
        name="nsa_sample_topk",
    )(imp)


def _nsa_s2_kernel(pt_ref, idx_ref, cache_ref, qr_ref, seln_ref, winn_ref, winb_ref, g_ref, ocmp_ref, o_ref,
                   buf, sem, *, base, past):
    b = pl.program_id(0)
    n_past = past // NSA_SEL
    per_page = PAGE_SIZE // NSA_SEL
    nk = NSA_TOPK * NSA_SEL

    def issue(step, slot, start):
        for k in range(NSA_TOPK):
            n = jnp.minimum(idx_ref[step, k], n_past - 1)
            page = pt_ref[step, n // per_page] + base
            row = pl.multiple_of((n % per_page) * NSA_SEL, NSA_SEL)
            cp = pltpu.make_async_copy(cache_ref.at[page, pl.ds(row, NSA_SEL)],
                                       buf.at[slot, pl.ds(k * NSA_SEL, NSA_SEL)], sem.at[slot])
            if start:
                cp.start()
            else:
                cp.wait()

    slot = _prefetch_schedule(b, pl.num_programs(0), issue)

    qr = qr_ref[0]
    lane = lax.broadcasted_iota(jnp.int32, (1, nk), 1)
    tok = jnp.zeros((1, nk), jnp.int32)
    is_past = jnp.zeros((1, nk), jnp.int32)
    tail_ok = jnp.int32(0)
    for k in range(NSA_TOPK):
        ik = idx_ref[b, k]
        in_k = lane // NSA_SEL == k
        tok = jnp.where(in_k, ik * NSA_SEL + lane % NSA_SEL, tok)
        is_past = jnp.where(in_k, (ik < n_past).astype(jnp.int32), is_past)
        tail_ok = tail_ok | ((ik >= n_past) & (ik * NSA_SEL <= past)).astype(jnp.int32)
    kv = buf[slot].astype(BF16)
    s_sel = jnp.where((is_past > 0) & (tok <= past), _dot_nt(qr, kv), NEG)
    s_new = jnp.where(tail_ok > 0, _rowdot(qr, seln_ref[0]), NEG)
    (e_sel, e_new), l = _softmax_parts([s_sel, s_new])
    o_sel = (_dot(e_sel.astype(BF16), kv) + e_new * seln_ref[0]) / l

    wb = winb_ref.shape[1]
    kvw = winb_ref[0].astype(BF16)
    dist = wb - lax.broadcasted_iota(jnp.int32, (1, wb), 1)
    s_win = jnp.where((dist >= 0) & (dist <= NSA_WINDOW) & (past - dist >= 0), _dot_nt(qr, kvw), NEG)
    s_wn = _rowdot(qr, winn_ref[0])
    (e_win, e_wn), lw = _softmax_parts([s_win, s_wn])
    o_win = (_dot(e_win.astype(BF16), kvw) + e_wn * winn_ref[0]) / lw

    g = g_ref[0]
    o = g[:, 0:1] * ocmp_ref[0] + g[:, 1:2] * o_sel[:, HEAD_DIM:] + g[:, 2:3] * o_win[:, HEAD_DIM:]
    o_ref[0] = o.astype(o_ref.dtype)


def _nsa_sample_attend(page_table, idx, cache, qr8, sel_new, win_new, win_buf, gates8, o_cmp, base, win_base, past):
    nb = qr8.shape[0]
    wb = win_buf.shape[1]

    def per_b(r, w):
        return pl.BlockSpec((1, r, w), lambda b, pt, ix: (b, 0, 0))

    grid_spec = pltpu.PrefetchScalarGridSpec(
        num_scalar_prefetch=2,
        grid=(nb,),
        in_specs=[
            pl.BlockSpec(memory_space=pl.ANY),
            per_b(N_HEADS, LANES), per_b(1, LANES), per_b(1, LANES),
            pl.BlockSpec((1, wb, LANES), lambda b, pt, ix: (win_base + b, 0, 0)),
            per_b(N_HEADS, LANES), per_b(N_HEADS, HEAD_DIM),
        ],
        out_specs=per_b(N_HEADS, HEAD_DIM),
        scratch_shapes=[pltpu.VMEM((2, NSA_TOPK * NSA_SEL, LANES), F32), pltpu.SemaphoreType.DMA((2,))],
    )
    return pl.pallas_call(
        functools.partial(_nsa_s2_kernel, base=base, past=past),
        out_shape=jax.ShapeDtypeStruct((nb, N_HEADS, HEAD_DIM), BF16),
        grid_spec=grid_spec,
        compiler_params=_cparams(("arbitrary",)),
        name="nsa_sample_attend",
    )(page_table, idx, cache, qr8, sel_new, win_new, win_buf, gates8, o_cmp)


def _mla_s_kernel(pt_ref, cache_ref, q_ref, new_ref, wuv_ref, o_ref, buf, sem, m_scr, l_scr, acc_scr,
                  *, base, past, pch):
    b = pl.program_id(0)
    c = pl.program_id(1)
    nch = pl.num_programs(1)
    step = b * nch + c
    rows = pch * PAGE_SIZE
    width = cache_ref.shape[2]

    def issue(st, slot, start):
        sb = st // nch
        sc = st % nch

        def body(j, carry):
            page = pt_ref[sb, sc * pch + j] + base
            dst = buf.at[slot, pl.ds(pl.multiple_of(j * PAGE_SIZE, PAGE_SIZE), PAGE_SIZE)]
            cp = pltpu.make_async_copy(cache_ref.at[page], dst, sem.at[slot])
            if start:
                cp.start()
            else:
                cp.wait()
            return carry

        lax.fori_loop(0, pch, body, 0)

    slot = _prefetch_schedule(step, pl.num_programs(0) * nch, issue)

    @pl.when(c == 0)
    def _():
        m_scr[...] = jnp.full_like(m_scr, NEG)
        l_scr[...] = jnp.zeros_like(l_scr)
        acc_scr[...] = jnp.zeros_like(acc_scr)

    q = q_ref[0][:, :width]
    lat = buf[slot].astype(BF16)
    k_pos = c * rows + lax.broadcasted_iota(jnp.int32, (1, rows), 1)
    s = jnp.where(k_pos <= past, _dot_nt(q, lat), NEG)
    m, l, acc = _online_update(s, lat, m_scr[...], l_scr[...], acc_scr[...])
    m_scr[...] = m
    l_scr[...] = l
    acc_scr[...] = acc

    @pl.when(c == nch - 1)
    def _():
        new = new_ref[0][:, :width]
        s_new = _rowdot(q, new)
        m2 = jnp.maximum(m, s_new)
        alpha = jnp.exp(m - m2)
        e_new = jnp.exp(s_new - m2)
        o_lat = ((alpha * acc + e_new * new) / (alpha * l + e_new))[:, :MLA_DC].astype(BF16)
        full = _dot(o_lat, wuv_ref[...])
        head = lax.broadcasted_iota(jnp.int32, (N_HEADS, HEAD_DIM), 0)
        o = jnp.zeros((N_HEADS, HEAD_DIM), F32)
        for h in range(N_HEADS):
            o = jnp.where(head == h, full[:, h * HEAD_DIM:(h + 1) * HEAD_DIM], o)
        o_ref[0] = o.astype(o_ref.dtype)


def _mla_sample(page_table, cache, qc8, lat_new, wuv, base, past, pch):
    nb = qc8.shape[0]
    n_pages = past // PAGE_SIZE
    grid_spec = pltpu.PrefetchScalarGridSpec(
        num_scalar_prefetch=1,
        grid=(nb, n_pages // pch),
        in_specs=[
            pl.BlockSpec(memory_space=pl.ANY),
            pl.BlockSpec((1, N_HEADS, MLA_QW), lambda b, c, pt: (b, 0, 0)),
            pl.BlockSpec((1, 1, MLA_QW), lambda b, c, pt: (b, 0, 0)),
            pl.BlockSpec(wuv.shape, lambda b, c, pt: (0, 0)),
        ],
        out_specs=pl.BlockSpec((1, N_HEADS, HEAD_DIM), lambda b, c, pt: (b, 0, 0)),
        scratch_shapes=[pltpu.VMEM((2, pch * PAGE_SIZE, cache.shape[2]), F32), pltpu.SemaphoreType.DMA((2,)),
                        pltpu.VMEM((N_HEADS, 1), F32), pltpu.VMEM((N_HEADS, 1), F32),
                        pltpu.VMEM((N_HEADS, cache.shape[2]), F32)],
    )
    return pl.pallas_call(
        functools.partial(_mla_s_kernel, base=base, past=past, pch=pch),
        out_shape=jax.ShapeDtypeStruct((nb, N_HEADS, HEAD_DIM), BF16),
        grid_spec=grid_spec,
        compiler_params=_cparams(("arbitrary", "arbitrary")),
        name="mla_sample",
    )(page_table, cache, qc8, lat_new, wuv)


def _dot_split(x, w, parts):
    pieces = _split3(x)[:parts]
    out = _dot(pieces[0], w)
    for p in pieces[1:]:
        out = out + _dot(p, w)
    return out


def _fox_bias_kernel(pt_ref, cache_ref, u_ref, v_ref, e_ref, o_ref, buf, sem, *, base, past, group):
    i = pl.program_id(0)
    n_pages = past // PAGE_SIZE

    def issue(step, slot, start):
        for g in range(group):
            _page_dma(cache_ref, pt_ref, step * group + g, base,
                      lambda j, g=g: buf.at[slot, pl.ds(g * n_pages + j, 1)],
                      sem.at[slot], n_pages, start)

    slot = _prefetch_schedule(i, pl.num_programs(0), issue)
    x = buf[slot]
    within = _dot_split(x, u_ref[...], 3)
    tot = _dot_split(x, v_ref[...], 3)
    upper = (lax.broadcasted_iota(jnp.int32, (n_pages, n_pages), 1)
             > lax.broadcasted_iota(jnp.int32, (n_pages, n_pages), 0)).astype(BF16)
    later = jnp.concatenate(
        [_dot_split_lhs(upper, tot[g * n_pages:(g + 1) * n_pages]) for g in range(group)], axis=0)
    o_ref[...] = within + _dot_split(later, e_ref[...], 3)


def _dot_split_lhs(w, x):
    hi, mid, lo = _split3(x)
    return (_dot(w, hi) + _dot(w, mid)) + _dot(w, lo)


def _fox_sample_bias(page_table, cache, u, v, e, base, past, group):
    nb = page_table.shape[0]
    n_pages = past // PAGE_SIZE
    pw = cache.shape[2]
    grid_spec = pltpu.PrefetchScalarGridSpec(
        num_scalar_prefetch=1,
        grid=(nb // group,),
        in_specs=[
            pl.BlockSpec(memory_space=pl.ANY),
            pl.BlockSpec(u.shape, lambda i, pt: (0, 0)),
            pl.BlockSpec(v.shape, lambda i, pt: (0, 0)),
            pl.BlockSpec(e.shape, lambda i, pt: (0, 0)),
        ],
        out_specs=pl.BlockSpec((group * n_pages, pw), lambda i, pt: (i, 0)),
        scratch_shapes=[pltpu.VMEM((2, group * n_pages, pw), F32), pltpu.SemaphoreType.DMA((2,))],
    )
    return pl.pallas_call(
        functools.partial(_fox_bias_kernel, base=base, past=past, group=group),
        out_shape=jax.ShapeDtypeStruct((nb * n_pages, pw), F32),
        grid_spec=grid_spec,
        compiler_params=_cparams(("arbitrary",)),
        name="fox_sample_bias",
    )(page_table, cache, u, v, e)


def _fox_s_kernel(pt_ref, cache_ref, q_ref, bias_ref, new_ref, lnew_ref, o_ref, buf, sem, *, base, past):
    b = pl.program_id(0)
    n_pages = past // PAGE_SIZE

    def issue(step, slot, start):
        _page_dma(cache_ref, pt_ref, step, base,
                  lambda j: buf.at[slot, pl.ds(pl.multiple_of(j * PAGE_SIZE, PAGE_SIZE), PAGE_SIZE)],
                  sem.at[slot], n_pages, start)

    slot = _prefetch_schedule(b, pl.num_programs(0), issue)
    q = (q_ref[0] * ATT_SCALE).astype(BF16)
    kv = buf[slot].astype(BF16)
    bias =jnp.concatenate([bias_ref[j] for j in range(n_pages)], axis=1)
    k_pos = lax.broadcasted_iota(jnp.int32, (1, past), 1)
    s = jnp.where(k_pos <= past, _dot_nt(q, kv) + bias + lnew_ref[0][:, 0:1], NEG)
    s_new = _rowdot(q, new_ref[0])
    (e, e_new), l = _softmax_parts([s, s_new])
    o = (_dot(e.astype(BF16), kv) + e_new * new_ref[0]) / l
    o_ref[0] = o[:, HEAD_DIM:].astype(o_ref.dtype)


def _fox_sample(page_table, cache, q8, bias, kv_new, lnew, base, past):
    nb = q8.shape[0]
    n_pages = past // PAGE_SIZE

    def per_b(r, w):
        return pl.BlockSpec((1, r, w), lambda b, pt: (b, 0, 0))

    grid_spec = pltpu.PrefetchScalarGridSpec(
        num_scalar_prefetch=1,
        grid=(nb,),
        in_specs=[
            pl.BlockSpec(memory_space=pl.ANY),
            per_b(N_HEADS, LANES),
            pl.BlockSpec((n_pages, N_HEADS, PAGE_SIZE), lambda b, pt: (b, 0, 0)),
            per_b(1, LANES), per_b(N_HEADS, LANES),
        ],
        out_specs=per_b(N_HEADS, HEAD_DIM),
        scratch_shapes=[pltpu.VMEM((2, past, LANES), F32), pltpu.SemaphoreType.DMA((2,))],
    )
    return pl.pallas_call(
        functools.partial(_fox_s_kernel, base=base, past=past),
        out_shape=jax.ShapeDtypeStruct((nb, N_HEADS, HEAD_DIM), BF16),
        grid_spec=grid_spec,
        compiler_params=_cparams(("arbitrary",)),
        name="fox_sample",
    )(page_table, cache, q8, bias, kv_new, lnew)


def _gmlp_s_kernel(u_ref, v_ref, w_ref, b_ref, o_ref):
    o_ref[...] = (u_ref[...] * (v_ref[...] * w_ref[...] + b_ref[...])).astype(o_ref.dtype)


def _gmlp_sample(u, v, w00, b0, row0, nb):
    blk = row0 // nb
    return pl.pallas_call(
        _gmlp_s_kernel,
        out_shape=jax.ShapeDtypeStruct((nb, MIX_W), BF16),
        grid=(1,),
        in_specs=[pl.BlockSpec((nb, MIX_W), lambda i: (blk, 0)), pl.BlockSpec((nb, MIX_W), lambda i: (blk, 0)),
                  pl.BlockSpec((1, MIX_W), lambda i: (0, 0)), pl.BlockSpec((1, MIX_W), lambda i: (0, 0))],
        out_specs=pl.BlockSpec((nb, MIX_W), lambda i: (0, 0)),
        name="gmlp_sample",
    )(u, v, w00, b0)


def _lane_page_dma(cache_ref, pt_ref, b, base, buf, slot, sem, n_pages, start, page0=0):
    def body(j, c):
        dst = buf.at[slot, :, pl.ds(pl.multiple_of(j * PAGE_SIZE, PAGE_SIZE), PAGE_SIZE)]
        cp = pltpu.make_async_copy(cache_ref.at[pt_ref[b, page0 + j] + base], dst, sem)
        if start:
            cp.start()
        else:
            cp.wait()
        return c

    lax.fori_loop(0, n_pages, body, 0)


def _split2(x):
    hi = x.astype(BF16)
    return hi, (x - hi.astype(F32)).astype(BF16)


def _nsa_t1_kernel(pt_ref, cache_ref, q_ref, new_ref, w_ref, wt_ref, pool_ref, imp_ref, o_ref, buf, sem,
                   *, base, past):
    b = pl.program_id(0)
    n_pages = past // PAGE_SIZE
    gt, nbg = pool_ref.shape
    hb = nbg // 2

    def issue(step, slot, start):
        _lane_page_dma(cache_ref, pt_ref, step, base, buf, slot, sem.at[slot], n_pages, start)

    slot = _prefetch_schedule(b, pl.num_programs(0), issue)

    wt = jnp.concatenate([wt_ref[...]] * (gt // LANES), axis=1)
    pool = pool_ref[...]
    groups = []
    for g in range(past // gt):
        hi, mid = _split2(buf[slot, :, g * gt:(g + 1) * gt] * wt)
        groups.append((_dot(hi, pool) + _dot(mid, pool)) * (1.0 / NSA_CMP))
    cmp_t = jnp.concatenate(groups, axis=1).astype(BF16)
    te = new_ref[0] * w_ref[0:1, :] * (1.0 / NSA_CMP)
    to = jnp.zeros_like(te)

    q = q_ref[0]
    lane = lax.broadcasted_iota(jnp.int32, (1, past // NSA_CMP), 1)
    blk = (lane // nbg) * nbg + 2 * (lane % hb) + (lane % nbg) // hb
    ok = blk * NSA_CMP + NSA_CMP - 1 <= past
    ns = past // NSA_SEL
    ok_te = ns * NSA_SEL + NSA_CMP - 1 <= past
    ok_to = ns * NSA_SEL + NSA_SEL - 1 <= past
    s = jnp.where(ok, _dot(q, cmp_t), NEG)
    s_te = jnp.where(ok_te, _rowdot(q, te), NEG)
    s_to = jnp.where(ok_to, _rowdot(q, to), NEG)
    (e, e_te, e_to), l = _softmax_parts([s, s_te, s_to])
    p = e / l * jnp.where(ok, 1.0, 0.0)
    p_te = e_te / l * (1.0 if ok_te else 0.0)
    p_to = e_to / l * (1.0 if ok_to else 0.0)
    o = _dot_nt(p.astype(BF16), cmp_t) + p_te * te + p_to * to
    o_ref[0] = o[:, HEAD_DIM:]
    ph = jnp.sum(p, axis=0, keepdims=True)
    imp = [ph[:, g * nbg:g * nbg + hb] + ph[:, g * nbg + hb:(g + 1) * nbg] for g in range(past // gt)]
    imp_tail = jnp.sum(p_te + p_to, axis=0, keepdims=True)
    imp_ref[0] = jnp.concatenate(imp + [jnp.broadcast_to(imp_tail, (1, LANES))], axis=1)


def _nsa_sample_cmp_t(page_table, cache, q8, kv_new, w, wt, pool, base, past):
    nb = q8.shape[0]
    ns = past // NSA_SEL
    grid_spec = pltpu.PrefetchScalarGridSpec(
        num_scalar_prefetch=1,
        grid=(nb,),
        in_specs=[
            pl.BlockSpec(memory_space=pl.ANY),
            pl.BlockSpec((1, N_HEADS, LANES), lambda b, pt: (b, 0, 0)),
            pl.BlockSpec((1, 1, LANES), lambda b, pt: (b, 0, 0)),
            pl.BlockSpec(w.shape, lambda b, pt: (0, 0)),
            pl.BlockSpec(wt.shape, lambda b, pt: (0, 0)),
            pl.BlockSpec(pool.shape, lambda b, pt: (0, 0)),
        ],
        out_specs=(pl.BlockSpec((1, 1, ns + LANES), lambda b, pt: (b, 0, 0)),
                   pl.BlockSpec((1, N_HEADS, HEAD_DIM), lambda b, pt: (b, 0, 0))),
        scratch_shapes=[pltpu.VMEM((2, LANES, past), F32), pltpu.SemaphoreType.DMA((2,))],
    )
    return pl.pallas_call(
        functools.partial(_nsa_t1_kernel, base=base, past=past),
        out_shape=(jax.ShapeDtypeStruct((nb, 1, ns + LANES), F32),
                   jax.ShapeDtypeStruct((nb, N_HEADS, HEAD_DIM), F32)),
        grid_spec=grid_spec,
        compiler_params=_cparams(("arbitrary",)),
        name="nsa_sample_cmp",
    )(page_table, cache, q8, kv_new, w, wt, pool)


def _nsa_t2_kernel(pt_ref, idx_ref, cache_ref, qr_ref, seln_ref, winn_ref, winb_ref, g_ref, ocmp_ref, o_ref,
                   buf, sem, *, base, past):
    b = pl.program_id(0)
    n_past = past // NSA_SEL
    per_page = PAGE_SIZE // NSA_SEL
    nk = NSA_TOPK * PAGE_SIZE

    def issue(step, slot, start):
        for k in range(NSA_TOPK):
            n = jnp.minimum(idx_ref[step, k], n_past - 1)
            page = pt_ref[step, n // per_page] + base
            cp = pltpu.make_async_copy(cache_ref.at[page], buf.at[slot, :, pl.ds(k * PAGE_SIZE, PAGE_SIZE)],
                                       sem.at[slot])
            if start:
                cp.start()
            else:
                cp.wait()

    slot = _prefetch_schedule(b, pl.num_programs(0), issue)

    qr = qr_ref[0]
    lane = lax.broadcasted_iota(jnp.int32, (1, nk), 1)
    in_page = lane % PAGE_SIZE
    tok = jnp.zeros((1, nk), jnp.int32)
    valid = jnp.zeros((1, nk), jnp.int32)
    tail_ok = jnp.int32(0)
    for k in range(NSA_TOPK):
        ik = idx_ref[b, k]
        n = jnp.minimum(ik, n_past - 1)
        in_k = lane // PAGE_SIZE == k
        tok = jnp.where(in_k, (n // per_page) * PAGE_SIZE + in_page, tok)
        mine = jnp.where(in_page // NSA_SEL == n % per_page, (ik < n_past).astype(jnp.int32), 0)
        valid = jnp.where(in_k, mine, valid)
        tail_ok = tail_ok | ((ik >= n_past) & (ik * NSA_SEL <= past)).astype(jnp.int32)
    kv = buf[slot].astype(BF16)
    s_sel = jnp.where(jnp.where(tok <= past, valid, 0) > 0, _dot(qr, kv), NEG)
    s_new = jnp.where(tail_ok > 0, _rowdot(qr, seln_ref[0]), NEG)
    (e_sel, e_new), l = _softmax_parts([s_sel, s_new])
    o_sel = (_dot_nt(e_sel.astype(BF16), kv) + e_new * seln_ref[0]) / l

    wb = winb_ref.shape[2]
    kvw = winb_ref[0].astype(BF16)
    dist = wb - lax.broadcasted_iota(jnp.int32, (1, wb), 1)
    in_win = jnp.where(dist <= NSA_WINDOW, jnp.where(past - dist >= 0, 1, 0), 0)
    s_win = jnp.where(in_win > 0, _dot(qr, kvw), NEG)
    s_wn = _rowdot(qr, winn_ref[0])
    (e_win, e_wn), lw = _softmax_parts([s_win, s_wn])
    o_win = (_dot_nt(e_win.astype(BF16), kvw) + e_wn * winn_ref[0]) / lw

    g = g_ref[0]
    o = g[:, 0:1] * ocmp_ref[0] + g[:, 1:2] * o_sel[:, HEAD_DIM:] + g[:, 2:3] * o_win[:, HEAD_DIM:]
    o_ref[0] = o.astype(o_ref.dtype)


def _nsa_sample_attend_t(page_table, idx, cache, qr8, sel_new, win_new, win_t, gates8, o_cmp, base, win_base, past):
    nb = qr8.shape[0]
    wb = win_t.shape[2]

    def per_b(r, w):
        return pl.BlockSpec((1, r, w), lambda b, pt, ix: (b, 0, 0))

    grid_spec = pltpu.PrefetchScalarGridSpec(
        num_scalar_prefetch=2,
        grid=(nb,),
        in_specs=[
            pl.BlockSpec(memory_space=pl.ANY),
            per_b(N_HEADS, LANES), per_b(1, LANES), per_b(1, LANES),
            pl.BlockSpec((1, LANES, wb), lambda b, pt, ix: (win_base + b, 0, 0)),
            per_b(N_HEADS, LANES), per_b(N_HEADS, HEAD_DIM),
        ],
        out_specs=per_b(N_HEADS, HEAD_DIM),
        scratch_shapes=[pltpu.VMEM((2, LANES, NSA_TOPK * PAGE_SIZE), F32), pltpu.SemaphoreType.DMA((2,))],
    )
    return pl.pallas_call(
        functools.partial(_nsa_t2_kernel, base=base, past=past),
        out_shape=jax.ShapeDtypeStruct((nb, N_HEADS, HEAD_DIM), BF16),
        grid_spec=grid_spec,
        compiler_params=_cparams(("arbitrary",)),
        name="nsa_sample_attend",
    )(page_table, idx, cache, qr8, sel_new, win_new, win_t, gates8, o_cmp)


def _mla_t_kernel(pt_ref, cache_ref, q_ref, new_ref, wuv_ref, o_ref, buf, sem, m_scr, l_scr, acc_scr,
                  *, base, past, pch):
    b = pl.program_id(0)
    c = pl.program_id(1)
    nch = pl.num_programs(1)
    step = b * nch + c
    cols = pch * PAGE_SIZE
    width = cache_ref.shape[1]

    def issue(st, slot, start):
        _lane_page_dma(cache_ref, pt_ref, st // nch, base, buf, slot, sem.at[slot], pch, start,
                       page0=(st % nch) * pch)

    slot = _prefetch_schedule(step, pl.num_programs(0) * nch, issue)

    @pl.when(c == 0)
    def _():
        m_scr[...] = jnp.full_like(m_scr, NEG)
        l_scr[...] = jnp.zeros_like(l_scr)
        acc_scr[...] = jnp.zeros_like(acc_scr)

    q = q_ref[0]
    lat_t = buf[slot].astype(BF16)
    k_pos = c * cols + lax.broadcasted_iota(jnp.int32, (1, cols), 1)
    s = jnp.where(k_pos <= past, _dot(q[:, :width], lat_t), NEG)
    m_old = m_scr[...]
    m = jnp.maximum(m_old, jnp.max(s, axis=-1, keepdims=True))
    alpha = jnp.exp(m_old - m)
    p = jnp.exp(s - m)
    l = alpha * l_scr[...] + jnp.sum(p, axis=-1, keepdims=True)
    acc = alpha * acc_scr[...] + _dot_nt(p.astype(BF16), lat_t[:MLA_DC])
    m_scr[...] = m
    l_scr[...] = l
    acc_scr[...] = acc

    @pl.when(c == nch - 1)
    def _():
        new = new_ref[0]
        s_new = _rowdot(q, new)
        m2 = jnp.maximum(m, s_new)
        a2 = jnp.exp(m - m2)
        e_new = jnp.exp(s_new - m2)
        o_lat = ((a2 * acc + e_new * new[:, :MLA_DC]) / (a2 * l + e_new)).astype(BF16)
        full = _dot(o_lat, wuv_ref[...])
        head = lax.broadcasted_iota(jnp.int32, (N_HEADS, HEAD_DIM), 0)
        o = jnp.zeros((N_HEADS, HEAD_DIM), F32)
        for h in range(N_HEADS):
            o = jnp.where(head == h, full[:, h * HEAD_DIM:(h + 1) * HEAD_DIM], o)
        o_ref[0] = o.astype(o_ref.dtype)


def _mla_sample_t(page_table, cache, qc8, lat_new, wuv, base, past, pch):
    nb = qc8.shape[0]
    n_pages = past // PAGE_SIZE
    width = cache.shape[1]
    grid_spec = pltpu.PrefetchScalarGridSpec(
        num_scalar_prefetch=1,
        grid=(nb, n_pages // pch),
        in_specs=[
            pl.BlockSpec(memory_space=pl.ANY),
            pl.BlockSpec((1, N_HEADS, MLA_QW), lambda b, c, pt: (b, 0, 0)),
            pl.BlockSpec((1, 1, MLA_QW), lambda b, c, pt: (b, 0, 0)),
            pl.BlockSpec(wuv.shape, lambda b, c, pt: (0, 0)),
        ],
        out_specs=pl.BlockSpec((1, N_HEADS, HEAD_DIM), lambda b, c, pt: (b, 0, 0)),
        scratch_shapes=[pltpu.VMEM((2, width, pch * PAGE_SIZE), F32), pltpu.SemaphoreType.DMA((2,)),
                        pltpu.VMEM((N_HEADS, 1), F32), pltpu.VMEM((N_HEADS, 1), F32),
                        pltpu.VMEM((N_HEADS, MLA_DC), F32)],
    )
    return pl.pallas_call(
        functools.partial(_mla_t_kernel, base=base, past=past, pch=pch),
        out_shape=jax.ShapeDtypeStruct((nb, N_HEADS, HEAD_DIM), BF16),
        grid_spec=grid_spec,
        compiler_params=_cparams(("arbitrary", "arbitrary")),
        name="mla_sample",
    )(page_table, cache, qc8, lat_new, wuv)


def _fox_t_kernel(pt_ref, cache_ref, lcache_ref, q_ref, new_ref, lnew_ref, u_ref, ones_ref, m_ref, o_ref,
                  buf, lbuf, sem, lsem, *, base, past):
    b = pl.program_id(0)
    n_pages = past // PAGE_SIZE

    def issue(step, slot, start):
        _lane_page_dma(cache_ref, pt_ref, step, base, buf, slot, sem.at[slot], n_pages, start)
        _page_dma(lcache_ref, pt_ref, step, base,
                  lambda j: lbuf.at[slot, pl.ds(pl.multiple_of(j * N_HEADS, N_HEADS), N_HEADS)],
                  lsem.at[slot], n_pages, start)

    slot = _prefetch_schedule(b, pl.num_programs(0), issue)

    x = lbuf[slot]
    within = _dot_split(x, u_ref[...], 3)
    totals = _dot_split(x, ones_ref[...], 3)
    bias2 = within + _dot_split_lhs(m_ref[...], totals)
    bias = jnp.concatenate([bias2[j * N_HEADS:(j + 1) * N_HEADS] for j in range(n_pages)], axis=1)

    q = (q_ref[0] * ATT_SCALE).astype(BF16)
    kv = buf[slot].astype(BF16)
    k_pos = lax.broadcasted_iota(jnp.int32, (1, past), 1)
    s = jnp.where(k_pos <= past, _dot(q, kv) + bias + lnew_ref[0][:, 0:1], NEG)
    s_new = _rowdot(q, new_ref[0])
    (e, e_new), l = _softmax_parts([s, s_new])
    o = (_dot_nt(e.astype(BF16), kv) + e_new * new_ref[0]) / l
    o_ref[0] = o[:, HEAD_DIM:].astype(o_ref.dtype)


def _fox_sample_t(page_table, cache, lcache, q8, kv_new, lnew, u, ones, later, base, past):
    nb = q8.shape[0]
    n_pages = past // PAGE_SIZE

    def per_b(r, w):
        return pl.BlockSpec((1, r, w), lambda b, pt: (b, 0, 0))

    def const(a):
        return pl.BlockSpec(a.shape, lambda b, pt: (0, 0))

    grid_spec = pltpu.PrefetchScalarGridSpec(
        num_scalar_prefetch=1,
        grid=(nb,),
        in_specs=[
            pl.BlockSpec(memory_space=pl.ANY), pl.BlockSpec(memory_space=pl.ANY),
            per_b(N_HEADS, LANES), per_b(1, LANES), per_b(N_HEADS, LANES),
            const(u), const(ones), const(later),
        ],
        out_specs=per_b(N_HEADS, HEAD_DIM),
        scratch_shapes=[pltpu.VMEM((2, LANES, past), F32), pltpu.VMEM((2, n_pages * N_HEADS, PAGE_SIZE), F32),
                        pltpu.SemaphoreType.DMA((2,)), pltpu.SemaphoreType.DMA((2,))],
    )
    return pl.pallas_call(
        functools.partial(_fox_t_kernel, base=base, past=past),
        out_shape=jax.ShapeDtypeStruct((nb, N_HEADS, HEAD_DIM), BF16),
        grid_spec=grid_spec,
        compiler_params=_cparams(("arbitrary",)),
        name="fox_sample",
    )(page_table, cache, lcache, q8, kv_new, lnew, u, ones, later)


def _kv_pages_t(cache):
    l, p, t = cache.shape[:3]
    return jnp.transpose(cache, (0, 1, 3, 4, 5, 2)).reshape(l * p, -1, t)


def _pages_t(cache):
    l, p, t, c = cache.shape
    return jnp.transpose(cache, (0, 1, 3, 2)).reshape(l * p, c, t)


def _sample_consts(past):
    n_pages = past // PAGE_SIZE
    gt = min(past, 32 * PAGE_SIZE)
    nbg = gt // NSA_CMP
    n = np.arange(gt) // NSA_CMP
    col = (n % 2) * (nbg // 2) + n // 2
    pool = col[:, None] == np.arange(nbg)[None, :]
    t = np.arange(PAGE_SIZE)
    u = t[:, None] > t[None, :]
    ones = np.ones((PAGE_SIZE, PAGE_SIZE), bool)
    j, h = np.divmod(np.arange(n_pages * N_HEADS), N_HEADS)
    later = (h[:, None] == h[None, :]) & (j[None, :] > j[:, None])
    return tuple(jnp.asarray(a, dtype=BF16) for a in (pool, u, ones, later))


def _swap_rot(w, rot):
    half = rot // 2
    return jnp.concatenate([-w[..., half:rot], w[..., :half], jnp.zeros_like(w[..., rot:])], axis=-1)


def _pad_last(w, n):
    return jnp.pad(w, [(0, 0)] * (w.ndim - 1) + [(0, n - w.shape[-1])])


def _small_proj_weight(w_in):
    d = w_in.shape[0]
    off = [0]

    def take(n):
        s = w_in[:, off[0]:off[0] + n]
        off[0] += n
        return s

    nsa_q = take(MIX_W)
    nsa_kv = take(6 * HEAD_DIM)
    nsa_g = take(3 * N_HEADS)
    gm_u, gm_v, mla_cq = take(MIX_W), take(MIX_W), take(MLA_DQ)
    mla_ckv, mla_kr = take(MLA_DC), take(MLA_DR)
    fox_q, fox_kv, fox_f = take(MIX_W), take(2 * HEAD_DIM), take(N_HEADS)
    nsa_qs = _swap_rot(nsa_q.reshape(d, N_HEADS, HEAD_DIM), ROT_DIM).reshape(d, MIX_W)
    sel_ks = _swap_rot(nsa_kv[:, 2 * HEAD_DIM:3 * HEAD_DIM], ROT_DIM)
    win_ks = _swap_rot(nsa_kv[:, 4 * HEAD_DIM:5 * HEAD_DIM], ROT_DIM)
    segs = [nsa_q, nsa_qs, gm_u, gm_v, mla_cq, fox_q,
            nsa_kv[:, :LANES], nsa_kv[:, LANES:2 * LANES], nsa_kv[:, 2 * LANES:],
            _pad_last(sel_ks, LANES), _pad_last(win_ks, LANES), _pad_last(nsa_g, LANES),
            mla_ckv, _pad_last(mla_kr, LANES), _pad_last(_swap_rot(mla_kr, MLA_DR), LANES),
            fox_kv, _pad_last(fox_f, LANES)]
    w_small = jnp.concatenate(segs, axis=1).astype(BF16)
    assert w_small.shape[1] == Z_W
    return w_small, w_in[:, off[0]:].astype(BF16)


def _block_diag(w):
    h, r, c = w.shape
    eye = jnp.eye(h, dtype=bool)
    return jnp.where(eye[:, None, :, None], w[:, :, None, :], 0).reshape(h * r, h * c)


def _mla_weights(wq, wuk, wuv):
    dq = wq.shape[0]
    wq = wq.reshape(dq, N_HEADS, HEAD_DIM + MLA_DR)
    rope = wq[:, :, HEAD_DIM:]
    lead = jnp.zeros((dq, N_HEADS, MLA_DC), wq.dtype)
    wa = _pad_last(jnp.concatenate([lead, rope], axis=-1), MLA_QW).reshape(dq, N_HEADS * MLA_QW)
    wb = _pad_last(jnp.concatenate([lead, _swap_rot(rope, MLA_DR)], axis=-1), MLA_QW).reshape(dq, N_HEADS * MLA_QW)
    wqn = wq[:, :, :HEAD_DIM].reshape(dq, MIX_W)
    wbd = _block_diag(_pad_last(jnp.transpose(wuk, (1, 2, 0)), MLA_QW))
    wuv_bd = _block_diag(jnp.transpose(wuv, (1, 0, 2)))
    return (wqn.astype(BF16), wbd.astype(BF16), wa.astype(BF16), wb.astype(BF16), wuv_bd.astype(BF16),
            wuv.reshape(MLA_DC, MIX_W).astype(BF16))


def _rope_tables(pos):
    def cs(half, theta):
        inv = theta ** (-jnp.arange(half, dtype=F32) / half)
        ang = pos.astype(F32)[:, None] * inv[None, :]
        return jnp.cos(ang), jnp.sin(ang)

    n = pos.shape[0]
    c, s = cs(ROT_DIM // 2, ROPE_THETA)
    one, zero = jnp.ones((n, HEAD_DIM - ROT_DIM), F32), jnp.zeros((n, HEAD_DIM - ROT_DIM), F32)
    hc = jnp.concatenate([c, c, one], axis=1)
    hs = jnp.concatenate([s, s, zero], axis=1)
    tab_nsa = jnp.concatenate([hc, hc, hs, hs, hc, jnp.ones((n, HEAD_DIM), F32), hs, jnp.zeros((n, HEAD_DIM), F32)],
                              axis=1)
    c, s = cs(MLA_DR // 2, MLA_THETA)
    tail = jnp.zeros((n, MLA_QW - MLA_DC - MLA_DR), F32)
    tab_mla = jnp.concatenate([jnp.ones((n, MLA_DC), F32), c, c, tail, jnp.zeros((n, MLA_DC), F32), s, s, tail], axis=1)
    return tab_nsa, tab_mla


def _fox_bias_consts():
    t1, h1 = np.divmod(np.arange(PAGE_SIZE * N_HEADS), N_HEADS)
    h2, t2 = np.divmod(np.arange(PAGE_SIZE * N_HEADS), PAGE_SIZE)
    u = (h1[:, None] == h2[None, :]) & (t1[:, None] > t2[None, :])
    copies = LANES // N_HEADS
    hv, rv = np.divmod(np.arange(LANES), copies)
    v = h1[:, None] == hv[None, :]
    e = (hv[:, None] == h2[None, :]) & (rv[:, None] == 0)
    return tuple(jnp.asarray(a, dtype=BF16) for a in (u, v, e))


def kernel(x_prompt, x_sample, cache_nsa_cmp, cache_nsa_sel, state_nsa_win, cache_mla, cache_fox_kv, cache_fox_logf, page_table, p_prompt, p_sample, n_ffn1, ffn1_in, ffn1_out, n_mix, w_in, nsa_cmp_w, gm_norm, gm_ws, gm_bs, mla_q_norm, mla_wq, mla_kv_norm, mla_wuk, mla_wuv, fox_bf, w_br, w_o, n_ffn2, ffn2_in, ffn2_out, n_ple, ple_wg, ple_wp, norm_final):
    batch, seq, d = x_prompt.shape
    nb = x_sample.shape[0]
    depth = w_in.shape[0]
    n_pool = cache_mla.shape[1]
    past = page_table.shape[1] * PAGE_SIZE
    wb = state_nsa_win.shape[2]
    n_p = batch * seq
    n = n_p + nb
    tm = 384 if n % 384 == 0 else 128
    tq, tk = 128, 256

    x = jnp.concatenate([x_prompt.reshape(n_p, d), x_sample.reshape(nb, d)], axis=0)
    pos = jnp.concatenate([jnp.arange(n_p, dtype=jnp.int32) % seq, jnp.full((nb,), past, jnp.int32)])
    tab_nsa, tab_mla = _rope_tables(pos)
    pool_m, bias_u, bias_ones, bias_later = _sample_consts(past)
    c_nsa_cmp = _kv_pages_t(cache_nsa_cmp)
    c_nsa_sel = _kv_pages_t(cache_nsa_sel)
    c_mla = _pages_t(cache_mla)
    c_fox_kv = _kv_pages_t(cache_fox_kv)
    c_fox_logf = _pages_t(cache_fox_logf)
    win_t = jnp.transpose(state_nsa_win, (0, 1, 3, 4, 5, 2))
    win_state = win_t.reshape(depth * nb, LANES, wb)

    st_p, st_s = [], []
    for l in range(depth):
        w_small, w_gate = _small_proj_weight(w_in[l])
        wqn, wbd, wa, wb_, wuv_bd, wuv_flat = _mla_weights(mla_wq[l], mla_wuk[l], mla_wuv[l])
        cmp_w = nsa_cmp_w[l].reshape(NSA_CMP, LANES)
        bs_exp = jnp.repeat(gm_bs[l].T, HEAD_DIM, axis=1)
        w00 = jnp.repeat(gm_ws[l][:, 0, 0], HEAD_DIM).reshape(1, MIX_W)
        b0 = jnp.repeat(gm_bs[l][:, 0], HEAD_DIM).reshape(1, MIX_W)

        x = _ffn(x, n_ffn1[l], ffn1_in[l].astype(BF16), ffn1_out[l].astype(BF16), tm, 512)
        z = _proj(x, n_mix[l], w_small, tm)
        q, qr, kv_sel, kv_win, gates, gm_u, gm_v, qc, lat, logf = _prep(
            z, tab_nsa, tab_mla, gm_norm[l], mla_q_norm[l], mla_kv_norm[l],
            _pad_last(fox_bf[l].reshape(1, N_HEADS), LANES), wqn, wbd, wa, wb_, tm)
        kv_cmp = z[:, Z_NSA_CMP:Z_NSA_CMP + LANES]
        fox_kv = z[:, Z_FOX_KV:Z_FOX_KV + LANES]

        cmp_e, cmp_o = _nsa_compress(kv_cmp.reshape(n // NSA_SEL, NSA_SEL * LANES), cmp_w,
                                     n_p // NSA_SEL, seq // NSA_SEL)
        o_nsa = _nsa_prompt(q, qr, gates, cmp_e, cmp_o, kv_sel, kv_win, batch, seq, tq, tk)
        o_gm = _gmlp_prompt(gm_u, gm_v, gm_ws[l], bs_exp, n_p, 512)
        o_mla = _mla_prompt(qc, lat, wuv_bd, batch, seq, tq, tk)
        c_col, c_row = _fox_cumsum(logf, batch, seq)
        o_fox = _fox_prompt(z, c_col, c_row, batch, seq, tq, tk)

        base = l * n_pool
        q8 = _pad_last(q[n_p:].reshape(nb, N_HEADS, HEAD_DIM), LANES)
        qr8 = _pad_last(qr[n_p:].reshape(nb, N_HEADS, HEAD_DIM), LANES)
        cmp_wt = jnp.tile(cmp_w.T, (1, LANES // NSA_CMP))
        imp, o_cmp = _nsa_sample_cmp_t(page_table, c_nsa_cmp, q8, kv_cmp[n_p:].reshape(nb, 1, LANES), cmp_w, cmp_wt,
                                       pool_m, base, past)
        idx = _nsa_sample_topk(imp.reshape(nb, -1), past // NSA_SEL)
        gates8 = _pad_last(gates[n_p:, :3 * N_HEADS].reshape(nb, N_HEADS, 3), LANES)
        o_nsa_s = _nsa_sample_attend_t(page_table, idx, c_nsa_sel, qr8, kv_sel[n_p:].reshape(nb, 1, LANES),
                                       kv_win[n_p:].reshape(nb, 1, LANES), win_state, gates8, o_cmp,
                                       base, l * nb, past)
        o_gm_s = _gmlp_sample(gm_u, gm_v, w00, b0, n_p, nb)
        o_mla_s = _mla_sample_t(page_table, c_mla, qc[n_p:].reshape(nb, N_HEADS, MLA_QW),
                                lat[n_p:].reshape(nb, 1, MLA_QW), wuv_flat, base, past, min(64, past // PAGE_SIZE))
        fq8 = _pad_last(z[n_p:, Z_FOX_Q:Z_FOX_Q + MIX_W].reshape(nb, N_HEADS, HEAD_DIM), LANES)
        lnew = jnp.broadcast_to(logf[n_p:, :N_HEADS, None], (nb, N_HEADS, LANES))
        o_fox_s = _fox_sample_t(page_table, c_fox_kv, c_fox_logf, fq8, fox_kv[n_p:].reshape(nb, 1, LANES), lnew,
                                bias_u, bias_ones, bias_later, base, past)

        outs = [jnp.concatenate([a, b.reshape(nb, MIX_W)], axis=0)
                for a, b in ((o_nsa, o_nsa_s), (o_gm, o_gm_s), (o_mla, o_mla_s), (o_fox, o_fox_s))]
        x = _merge(x, n_mix[l], outs, w_gate, w_br[l].astype(BF16), w_o[l].astype(BF16), tm)
        x = _ffn(x, n_ffn2[l], ffn2_in[l].astype(BF16), ffn2_out[l].astype(BF16), tm, 512)
        p_all = jnp.concatenate([p_prompt[l].reshape(n_p, -1), p_sample[l].reshape(nb, -1)], axis=0)
        x = _ple(x, n_ple[l], p_all, ple_wg[l].astype(BF16), ple_wp[l].astype(BF16), norm_final,
                 l == depth - 1, tm)

        def kv5(a, rows):
            return a.reshape(rows, -1, 2, 1, HEAD_DIM)

        win_p = kv5(kv_win[:n_p], batch)
        st_p.append((kv5(kv_cmp[:n_p], batch), kv5(kv_sel[:n_p], batch), win_p[:, seq - min(NSA_WINDOW, seq):],
                     lat[:n_p, :MLA_DC + MLA_DR].reshape(batch, seq, -1), kv5(fox_kv[:n_p], batch),
                     logf[:n_p, :N_HEADS].reshape(batch, seq, N_HEADS)))
        new_t = kv_win[n_p:].reshape(nb, 2, 1, HEAD_DIM, 1)
        win_s = jnp.transpose(jnp.concatenate([win_t[l][..., 1:], new_t], axis=-1), (0, 4, 1, 2, 3))
        st_s.append((kv5(kv_cmp[n_p:], nb), kv5(kv_sel[n_p:], nb), win_s,
                     lat[n_p:, :MLA_DC + MLA_DR].reshape(nb, 1, -1), kv5(fox_kv[n_p:], nb),
                     logf[n_p:, :N_HEADS].reshape(nb, 1, N_HEADS), gm_v[n_p:].reshape(nb, 1, N_HEADS, HEAD_DIM)))

    y = x
    outs_p = [jnp.stack(a) for a in zip(*st_p)]
    outs_s = [jnp.stack(a) for a in zip(*st_s)]
    return (y[:n_p].reshape(batch, seq, d), y[n_p:].reshape(nb, 1, d), *outs_p, *outs_s)
```

```python
import functools

import numpy as np
import jax
import jax.numpy as jnp
from jax import lax
from jax.experimental import pallas as pl
from jax.experimental.pallas import tpu as pltpu

D_MODEL = 2048
N_HEADS = 8
HEAD_DIM = 64
MIX_W = N_HEADS * HEAD_DIM
N_BRANCH = 4
D_FF = 2 * D_MODEL
PLE_DIM = 256
ROPE_THETA = 500000.0
ROT_DIM = HEAD_DIM // 4
NSA_CMP = 32
NSA_SEL = 64
NSA_TOPK = 16
NSA_WINDOW = 512
GM_CHUNK = 128
MLA_DQ = D_MODEL // 4
MLA_DC = D_MODEL // 16
MLA_DR = HEAD_DIM // 2
MLA_THETA = 10000.0
EPS = 1e-6
NEG = -1e30
FORCE = 1e4
ATT_SCALE = HEAD_DIM ** -0.5
MLA_SCALE = (HEAD_DIM + MLA_DR) ** -0.5
PAGE_SIZE = 128

F32 = jnp.float32
BF16 = jnp.bfloat16
LANES = 128
MLA_QW = 2 * LANES
VMEM_LIMIT = 56 * 1024 * 1024

Z_NSA_Q, Z_NSA_QS, Z_GM_U, Z_GM_V, Z_MLA_CQ, Z_FOX_Q = (i * MIX_W for i in range(6))
Z_NSA_CMP = 6 * MIX_W
Z_NSA_SEL = Z_NSA_CMP + LANES
Z_NSA_WIN = Z_NSA_SEL + LANES
Z_NSA_KSS = Z_NSA_WIN + LANES
Z_NSA_KSW = Z_NSA_KSS + LANES
Z_NSA_G = Z_NSA_KSW + LANES
Z_MLA_CKV = Z_NSA_G + LANES
Z_MLA_KR = Z_MLA_CKV + LANES
Z_MLA_KRS = Z_MLA_KR + LANES
Z_FOX_KV = Z_MLA_KRS + LANES
Z_FOX_F = Z_FOX_KV + LANES
Z_W = Z_FOX_F + LANES


def _cparams(sem):
    return pltpu.CompilerParams(dimension_semantics=sem, vmem_limit_bytes=VMEM_LIMIT)


def _const_spec(shape):
    nd = len(shape)
    return pl.BlockSpec(shape, lambda *_: (0,) * nd, pipeline_mode=pl.Buffered(1))


def _rms(x, g):
    return x * lax.rsqrt(jnp.mean(x * x, axis=-1, keepdims=True) + EPS) * g


def _dot(a, b):
    return jnp.dot(a, b, preferred_element_type=F32)


def _dot_nt(a, b):
    return lax.dot_general(a, b, (((1,), (1,)), ((), ())), preferred_element_type=F32)


def _ffn_kernel(x_ref, g_ref, wg_ref, wu_ref, wo_ref, o_ref, h_scr, acc_scr):
    k = pl.program_id(1)

    @pl.when(k == 0)
    def _():
        h_scr[...] = _rms(x_ref[...], g_ref[...]).astype(BF16)
        acc_scr[...] = jnp.zeros_like(acc_scr)

    h = h_scr[...]
    g = _dot(h, wg_ref[...])
    u = _dot(h, wu_ref[...])
    a = (g * jax.nn.sigmoid(g)) * u
    acc_scr[...] += _dot(a.astype(BF16), wo_ref[...])

    @pl.when(k == pl.num_programs(1) - 1)
    def _():
        o_ref[...] = x_ref[...] + 0.5 * acc_scr[...]


def _ffn(x, g, w_in, w_out, tm, tc):
    n, d = x.shape
    dff = w_out.shape[0]
    nc = dff // tc
    return pl.pallas_call(
        _ffn_kernel,
        out_shape=jax.ShapeDtypeStruct((n, d), F32),
        grid=(n // tm, nc),
        in_specs=[
            pl.BlockSpec((tm, d), lambda i, k: (i, 0)),
            pl.BlockSpec((1, d), lambda i, k: (0, 0)),
            pl.BlockSpec((d, tc), lambda i, k: (0, k)),
            pl.BlockSpec((d, tc), lambda i, k: (0, k + nc)),
            pl.BlockSpec((tc, d), lambda i, k: (k, 0)),
        ],
        out_specs=pl.BlockSpec((tm, d), lambda i, k: (i, 0)),
        scratch_shapes=[pltpu.VMEM((tm, d), BF16), pltpu.VMEM((tm, d), F32)],
        compiler_params=_cparams(("parallel", "arbitrary")),
        name="ffn",
    )(x, g.reshape(1, d), w_in, w_in, w_out)


def _proj_kernel(x_ref, g_ref, w_ref, o_ref):
    h = _rms(x_ref[...], g_ref[...]).astype(BF16)
    o_ref[...] = _dot(h, w_ref[...])


def _proj(x, g, w, tm):
    n, d = x.shape
    zw = w.shape[1]
    return pl.pallas_call(
        _proj_kernel,
        out_shape=jax.ShapeDtypeStruct((n, zw), F32),
        grid=(n // tm,),
        in_specs=[
            pl.BlockSpec((tm, d), lambda i: (i, 0)),
            pl.BlockSpec((1, d), lambda i: (0, 0)),
            _const_spec((d, zw)),
        ],
        out_specs=pl.BlockSpec((tm, zw), lambda i: (i, 0)),
        compiler_params=_cparams(("parallel",)),
        name="in_proj",
    )(x, g.reshape(1, d), w)


def _merge_kernel(x_ref, g_ref, o0_ref, o1_ref, o2_ref, o3_ref, wg_ref, wbr_ref, wo_ref,
                  out_ref, h_scr, acc_scr):
    s = pl.program_id(1)
    half = s % 2

    @pl.when(s == 0)
    def _():
        h_scr[...] = _rms(x_ref[...], g_ref[...]).astype(BF16)
        acc_scr[...] = jnp.zeros_like(acc_scr)

    gate = jax.nn.sigmoid(_dot(h_scr[...], wg_ref[...]))
    for b, o_ref in enumerate((o0_ref, o1_ref, o2_ref, o3_ref)):
        @pl.when(s // 2 == b)
        def _(o_ref=o_ref):
            acc_scr[half] += gate * _dot(o_ref[...], wbr_ref[0])

    @pl.when(s == pl.num_programs(1) - 1)
    def _():
        hd = wo_ref.shape[0] // 2
        y = _dot(acc_scr[0].astype(BF16), wo_ref[:hd, :])
        y += _dot(acc_scr[1].astype(BF16), wo_ref[hd:, :])
        out_ref[...] = x_ref[...] + y


def _merge(x, g, outs, w_gate, w_br, w_o, tm):
    n, d = x.shape
    hd = d // 2
    o_spec = pl.BlockSpec((tm, MIX_W), lambda i, s: (i, 0))
    return pl.pallas_call(
        _merge_kernel,
        out_shape=jax.ShapeDtypeStruct((n, d), F32),
        grid=(n // tm, 2 * N_BRANCH),
        in_specs=[
            pl.BlockSpec((tm, d), lambda i, s: (i, 0)),
            pl.BlockSpec((1, d), lambda i, s: (0, 0)),
            o_spec, o_spec, o_spec, o_spec,
            pl.BlockSpec((d, hd), lambda i, s: (0, s)),
            pl.BlockSpec((1, MIX_W, hd), lambda i, s: (s // 2, 0, s % 2)),
            _const_spec((d, d)),
        ],
        out_specs=pl.BlockSpec((tm, d), lambda i, s: (i, 0)),
        scratch_shapes=[pltpu.VMEM((tm, d), BF16), pltpu.VMEM((2, tm, hd), F32)],
        compiler_params=_cparams(("parallel", "arbitrary")),
        name="merge",
    )(x, g.reshape(1, d), *outs, w_gate, w_br, w_o)


def _ple_kernel(x_ref, g_ref, p_ref, wg_ref, wp_ref, gf_ref, o_ref, *, final):
    x = x_ref[...]
    gate = jax.nn.sigmoid(_dot(_rms(x, g_ref[...]).astype(BF16), wg_ref[...]))
    y = x + gate * _dot(p_ref[...].astype(BF16), wp_ref[...])
    if final:
        y = _rms(y, gf_ref[...])
    o_ref[...] = y


def _ple(x, g, p, w_g, w_p, g_final, final, tm):
    n, d = x.shape
    pd = p.shape[1]
    return pl.pallas_call(
        functools.partial(_ple_kernel, final=final),
        out_shape=jax.ShapeDtypeStruct((n, d), F32),
        grid=(n // tm,),
        in_specs=[
            pl.BlockSpec((tm, d), lambda i: (i, 0)),
            pl.BlockSpec((1, d), lambda i: (0, 0)),
            pl.BlockSpec((tm, pd), lambda i: (i, 0)),
            _const_spec((d, d)),
            _const_spec((pd, d)),
            pl.BlockSpec((1, d), lambda i: (0, 0)),
        ],
        out_specs=pl.BlockSpec((tm, d), lambda i: (i, 0)),
        compiler_params=_cparams(("parallel",)),
        name="ple",
    )(x, g.reshape(1, d), p, w_g, w_p, g_final.reshape(1, d))


def _gelu(x):
    return 0.5 * x * (1.0 + lax.erf(x * np.float32(np.sqrt(0.5))))


def _log_sigmoid(x):
    return jnp.minimum(x, 0.0) - jnp.log1p(jnp.exp(-jnp.abs(x)))


def _prep_kernel(zq_ref, zqs_ref, zu_ref, zv_ref, zcq_ref, zsel_ref, zwin_ref, zkss_ref, zksw_ref,
                 zg_ref, zckv_ref, zkr_ref, zkrs_ref, zf_ref, tn_ref, tm_ref,
                 gmn_ref, qn_ref, kvn_ref, bf_ref, wqn_ref, wbd_ref, wa_ref, wb_ref,
                 oq_ref, oqr_ref, osel_ref, owin_ref, ogate_ref, ou_ref, ov_ref, oqc_ref,
                 olat_ref, ologf_ref):
    tn = tn_ref[...]
    cos_q = jnp.concatenate([tn[:, 0:LANES]] * (MIX_W // LANES), axis=1)
    sin_q = jnp.concatenate([tn[:, LANES:2 * LANES]] * (MIX_W // LANES), axis=1)
    cos_k = tn[:, 2 * LANES:3 * LANES]
    sin_k = tn[:, 3 * LANES:4 * LANES]
    q = zq_ref[...]
    oq_ref[...] = (q * ATT_SCALE).astype(BF16)
    oqr_ref[...] = ((q * cos_q + zqs_ref[...] * sin_q) * ATT_SCALE).astype(BF16)
    osel_ref[...] = zsel_ref[...] * cos_k + zkss_ref[...] * sin_k
    owin_ref[...] = zwin_ref[...] * cos_k + zksw_ref[...] * sin_k
    ogate_ref[...] = jax.nn.sigmoid(zg_ref[...])

    ou_ref[...] = _gelu(zu_ref[...])
    gv = _gelu(zv_ref[...])
    gc = gv - jnp.mean(gv, axis=-1, keepdims=True)
    ov_ref[...] = gc * lax.rsqrt(jnp.mean(gc * gc, axis=-1, keepdims=True) + EPS) * gmn_ref[...]

    tm = tm_ref[...]
    cqn = _rms(zcq_ref[...], qn_ref[...]).astype(BF16)
    q_nope = _dot(cqn, wqn_ref[...]).astype(BF16)
    qa = _dot(q_nope, wbd_ref[...]) + _dot(cqn, wa_ref[...])
    qb = _dot(cqn, wb_ref[...])
    cos_m = jnp.concatenate([tm[:, :MLA_QW]] * N_HEADS, axis=1)
    sin_m = jnp.concatenate([tm[:, MLA_QW:]] * N_HEADS, axis=1)
    oqc_ref[...] = ((qa * cos_m + qb * sin_m) * MLA_SCALE).astype(BF16)
    c = _rms(zckv_ref[...], kvn_ref[...])
    kr = zkr_ref[...] * tm[:, LANES:2 * LANES] + zkrs_ref[...] * tm[:, MLA_QW + LANES:]
    olat_ref[...] = jnp.concatenate([c, kr], axis=1)
    ologf_ref[...] = _log_sigmoid(zf_ref[...] + bf_ref[...])


def _prep(z, tab_nsa, tab_mla, gm_norm, q_norm, kv_norm, fox_bf, wqn, wbd, wa, wb, tm):
    n = z.shape[0]

    def zb(off, w):
        return pl.BlockSpec((tm, w), lambda i, o=off // w: (i, o))

    def row(w):
        return pl.BlockSpec((tm, w), lambda i: (i, 0))

    def vec(w):
        return pl.BlockSpec((1, w), lambda i: (0, 0))

    z_specs = [zb(Z_NSA_Q, MIX_W), zb(Z_NSA_QS, MIX_W), zb(Z_GM_U, MIX_W), zb(Z_GM_V, MIX_W),
               zb(Z_MLA_CQ, MIX_W), zb(Z_NSA_SEL, LANES), zb(Z_NSA_WIN, LANES), zb(Z_NSA_KSS, LANES),
               zb(Z_NSA_KSW, LANES), zb(Z_NSA_G, LANES), zb(Z_MLA_CKV, LANES), zb(Z_MLA_KR, LANES),
               zb(Z_MLA_KRS, LANES), zb(Z_FOX_F, LANES)]
    qcw = N_HEADS * MLA_QW
    out_shape = (
        jax.ShapeDtypeStruct((n, MIX_W), BF16), jax.ShapeDtypeStruct((n, MIX_W), BF16),
        jax.ShapeDtypeStruct((n, LANES), F32), jax.ShapeDtypeStruct((n, LANES), F32),
        jax.ShapeDtypeStruct((n, LANES), F32),
        jax.ShapeDtypeStruct((n, MIX_W), F32), jax.ShapeDtypeStruct((n, MIX_W), F32),
        jax.ShapeDtypeStruct((n, qcw), BF16), jax.ShapeDtypeStruct((n, MLA_QW), F32),
        jax.ShapeDtypeStruct((n, LANES), F32),
    )
    out_specs = (row(MIX_W), row(MIX_W), row(LANES), row(LANES), row(LANES), row(MIX_W), row(MIX_W),
                 row(qcw), row(MLA_QW), row(LANES))
    return pl.pallas_call(
        _prep_kernel,
        out_shape=out_shape,
        grid=(n // tm,),
        in_specs=z_specs + [row(4 * LANES), row(2 * MLA_QW), vec(MIX_W), vec(MLA_DQ), vec(LANES), vec(LANES),
                            _const_spec(wqn.shape), _const_spec(wbd.shape), _const_spec(wa.shape),
                            _const_spec(wb.shape)],
        out_specs=out_specs,
        compiler_params=_cparams(("parallel",)),
        name="mixer_prep",
    )(*([z] * len(z_specs)), tab_nsa, tab_mla, gm_norm.reshape(1, -1), q_norm.reshape(1, -1),
      kv_norm.reshape(1, -1), fox_bf, wqn, wbd, wa, wb)


def _stack_heads(x, width):
    return jnp.concatenate([x[:, h * width:(h + 1) * width] for h in range(N_HEADS)], axis=0)


def _unstack_heads(x, t):
    return jnp.concatenate([x[h * t:(h + 1) * t] for h in range(N_HEADS)], axis=1)


def _tile_rows(x):
    return jnp.concatenate([x] * N_HEADS, axis=0)


def _online_update(s, v, m, l, acc):
    m_new = jnp.maximum(m, jnp.max(s, axis=-1, keepdims=True))
    alpha = jnp.exp(m - m_new)
    p = jnp.exp(s - m_new)
    l = alpha * l + jnp.sum(p, axis=-1, keepdims=True)
    acc = alpha * acc + _dot(p.astype(BF16), v)
    return m_new, l, acc


def _flash_init(rows, width):
    return (jnp.full((rows, 1), NEG, F32), jnp.zeros((rows, 1), F32), jnp.zeros((rows, width), F32))


def _gmlp_kernel(u_ref, v_ref, ws_ref, bs_ref, o_ref):
    c = GM_CHUNK
    tril = (lax.broadcasted_iota(jnp.int32, (c, c), 0) >= lax.broadcasted_iota(jnp.int32, (c, c), 1)).astype(F32)
    grp = lax.broadcasted_iota(jnp.int32, (c, MIX_W), 1) // HEAD_DIM
    ws = [(ws_ref[g] * tril).astype(BF16) for g in range(N_HEADS)]
    for ch in range(u_ref.shape[0] // c):
        v = v_ref[ch * c:(ch + 1) * c, :].astype(BF16)
        z = bs_ref[...]
        for g in range(N_HEADS):
            z = z + jnp.where(grp == g, _dot(ws[g], v), 0.0)
        o_ref[ch * c:(ch + 1) * c, :] = (u_ref[ch * c:(ch + 1) * c, :] * z).astype(o_ref.dtype)


def _gmlp_prompt(u, v, ws, bs_exp, n_rows, tq):
    return pl.pallas_call(
        _gmlp_kernel,
        out_shape=jax.ShapeDtypeStruct((n_rows, MIX_W), BF16),
        grid=(n_rows // tq,),
        in_specs=[
            pl.BlockSpec((tq, MIX_W), lambda i: (i, 0)),
            pl.BlockSpec((tq, MIX_W), lambda i: (i, 0)),
            _const_spec(ws.shape),
            _const_spec(bs_exp.shape),
        ],
        out_specs=pl.BlockSpec((tq, MIX_W), lambda i: (i, 0)),
        compiler_params=_cparams(("parallel",)),
        name="gmlp_prompt",
    )(u, v, ws, bs_exp)


def _split3(x):
    hi = x.astype(BF16)
    r = x - hi.astype(F32)
    mid = r.astype(BF16)
    lo = (r - mid.astype(F32)).astype(BF16)
    return hi, mid, lo


def _cumsum_kernel(x_ref, col_ref, row_ref, *, blk):
    tri = (lax.broadcasted_iota(jnp.int32, (blk, blk), 0)
           >= lax.broadcasted_iota(jnp.int32, (blk, blk), 1)).astype(BF16)
    carry = jnp.zeros((1, x_ref.shape[1]), F32)
    for c in range(x_ref.shape[0] // blk):
        hi, mid, lo = _split3(x_ref[c * blk:(c + 1) * blk, :])
        cs = (_dot(tri, hi) + _dot(tri, mid)) + _dot(tri, lo) + carry
        col_ref[c * blk:(c + 1) * blk, :] = cs
        row_ref[:, c * blk:(c + 1) * blk] = cs.T[:row_ref.shape[0], :]
        carry = cs[blk - 1:blk, :]


def _fox_cumsum(logf, batch, seq):
    return pl.pallas_call(
        functools.partial(_cumsum_kernel, blk=256),
        out_shape=(jax.ShapeDtypeStruct((batch * seq, LANES), F32),
                   jax.ShapeDtypeStruct((batch * N_HEADS, seq), F32)),
        grid=(batch,),
        in_specs=[pl.BlockSpec((seq, LANES), lambda b: (b, 0))],
        out_specs=(pl.BlockSpec((seq, LANES), lambda b: (b, 0)),
                   pl.BlockSpec((N_HEADS, seq), lambda b: (b, 0))),
        compiler_params=_cparams(("parallel",)),
        name="fox_cumsum",
    )(logf)


def _fox_kernel(q_ref, kv_ref, cc_ref, cr_ref, o_ref, *, tq, tk):
    i = pl.program_id(1)
    q0 = i * tq
    qs = _stack_heads((q_ref[...] * ATT_SCALE).astype(BF16), HEAD_DIM)
    cc = cc_ref[...]
    cq = jnp.concatenate([cc[:, h:h + 1] for h in range(N_HEADS)], axis=0)
    t_pos = q0 + lax.broadcasted_iota(jnp.int32, (tq, tk), 0)
    l_off = lax.broadcasted_iota(jnp.int32, (tq, tk), 1)

    def make_body(masked):
        def body(j, carry):
            k0 = pl.multiple_of(j * tk, tk)
            kv = kv_ref[pl.ds(k0, tk), :]
            s = _dot_nt(qs, kv[:, :HEAD_DIM].astype(BF16))
            ck = cr_ref[:, pl.ds(k0, tk)]
            parts = [s[h * tq:(h + 1) * tq] - ck[h:h + 1, :] for h in range(N_HEADS)]
            if masked:
                ok = (k0 + l_off) <= t_pos
                parts = [jnp.where(ok, p, NEG) for p in parts]
            return _online_update(jnp.concatenate(parts, axis=0) + cq, kv[:, HEAD_DIM:].astype(BF16), *carry)
        return body

    nfull = q0 // tk
    nkb = (q0 + tq - 1) // tk + 1
    carry = lax.fori_loop(0, nfull, make_body(False), _flash_init(N_HEADS * tq, HEAD_DIM))
    m, l, acc = lax.fori_loop(nfull, nkb, make_body(True), carry)
    o_ref[...] = _unstack_heads(acc / l, tq).astype(o_ref.dtype)


def _fox_prompt(z, c_col, c_row, batch, seq, tq, tk):
    nq = seq // tq
    return pl.pallas_call(
        functools.partial(_fox_kernel, tq=tq, tk=tk),
        out_shape=jax.ShapeDtypeStruct((batch * seq, MIX_W), BF16),
        grid=(batch, nq),
        in_specs=[
            pl.BlockSpec((tq, MIX_W), lambda b, i: (b * nq + i, Z_FOX_Q // MIX_W)),
            pl.BlockSpec((seq, LANES), lambda b, i: (b, Z_FOX_KV // LANES)),
            pl.BlockSpec((tq, LANES), lambda b, i: (b * nq + i, 0)),
            pl.BlockSpec((N_HEADS, seq), lambda b, i: (b, 0)),
        ],
        out_specs=pl.BlockSpec((tq, MIX_W), lambda b, i: (b * nq + i, 0)),
        compiler_params=_cparams(("parallel", "parallel")),
        name="fox_prompt",
    )(z, z, c_col, c_row)


def _mla_kernel(q_ref, lat_ref, wuv_ref, o_ref, *, tq, tk):
    i = pl.program_id(1)
    q0 = i * tq
    qs = _stack_heads(q_ref[...], MLA_QW)
    t_pos = q0 + lax.broadcasted_iota(jnp.int32, (tq, tk), 0)
    l_off = lax.broadcasted_iota(jnp.int32, (tq, tk), 1)

    def make_body(masked):
        def body(j, carry):
            k0 = pl.multiple_of(j * tk, tk)
            lat = lat_ref[pl.ds(k0, tk), :].astype(BF16)
            s = _dot_nt(qs, lat)
            if masked:
                s = jnp.where(_tile_rows((k0 + l_off) <= t_pos), s, NEG)
            return _online_update(s, lat[:, :MLA_DC], *carry)
        return body

    nfull = q0 // tk
    nkb = (q0 + tq - 1) // tk + 1
    carry = lax.fori_loop(0, nfull, make_body(False), _flash_init(N_HEADS * tq, MLA_DC))
    m, l, acc = lax.fori_loop(nfull, nkb, make_body(True), carry)
    o_lat = _unstack_heads(acc / l, tq).astype(BF16)
    o_ref[...] = _dot(o_lat, wuv_ref[...]).astype(o_ref.dtype)


def _mla_prompt(qc, lat, wuv_bd, batch, seq, tq, tk):
    nq = seq // tq
    return pl.pallas_call(
        functools.partial(_mla_kernel, tq=tq, tk=tk),
        out_shape=jax.ShapeDtypeStruct((batch * seq, MIX_W), BF16),
        grid=(batch, nq),
        in_specs=[
            pl.BlockSpec((tq, N_HEADS * MLA_QW), lambda b, i: (b * nq + i, 0)),
            pl.BlockSpec((seq, MLA_QW), lambda b, i: (b, 0)),
            _const_spec(wuv_bd.shape),
        ],
        out_specs=pl.BlockSpec((tq, MIX_W), lambda b, i: (b * nq + i, 0)),
        compiler_params=_cparams(("parallel", "parallel")),
        name="mla_prompt",
    )(qc, lat, wuv_bd)


def _cmp_kernel(x_ref, w_ref, e_ref, o_ref):
    half = NSA_CMP * LANES
    for part, out in ((0, e_ref), (1, o_ref)):
        acc = jnp.zeros(out.shape, F32)
        for j in range(NSA_CMP):
            lo = part * half + j * LANES
            acc = acc + x_ref[:, lo:lo + LANES] * w_ref[j:j + 1, :]
        out[...] = acc * (1.0 / NSA_CMP)


def _nsa_compress(kv_rows, w, n_rows, tr):
    return pl.pallas_call(
        _cmp_kernel,
        out_shape=(jax.ShapeDtypeStruct((n_rows, LANES), F32), jax.ShapeDtypeStruct((n_rows, LANES), F32)),
        grid=(n_rows // tr,),
        in_specs=[pl.BlockSpec((tr, NSA_SEL * LANES), lambda i: (i, 0)), _const_spec(w.shape)],
        out_specs=(pl.BlockSpec((tr, LANES), lambda i: (i, 0)), pl.BlockSpec((tr, LANES), lambda i: (i, 0))),
        compiler_params=_cparams(("parallel",)),
        name="nsa_compress",
    )(kv_rows, w)


def _topk_mask(v, k):
    n = v.shape[1]
    idx = lax.broadcasted_iota(jnp.int32, v.shape, 1)
    rank = jnp.zeros(v.shape, F32)
    for m in range(n):
        col = v[:, m:m + 1]
        rank = rank + jnp.where(idx > m, jnp.where(col >= v, 1.0, 0.0), jnp.where(col > v, 1.0, 0.0))
    return jnp.where(rank < k, 1.0, 0.0)


def _nsa_kernel(q_ref, qr_ref, g_ref, ce_ref, co_ref, sel_ref, win_ref, o_ref, selx_scr, *, tq, tk):
    i = pl.program_id(1)
    q0 = i * tq
    seq = sel_ref.shape[0]
    ns = seq // NSA_SEL
    rows = N_HEADS * tq
    qs = _stack_heads(q_ref[...], HEAD_DIM)
    qrs = _stack_heads(qr_ref[...], HEAD_DIM)
    pos1 = q0 + lax.broadcasted_iota(jnp.int32, (tq, 1), 0)
    pos = _tile_rows(pos1)

    ce = ce_ref[...]
    co = co_ref[...]
    kc = jnp.concatenate([ce[:, :HEAD_DIM], co[:, :HEAD_DIM]], axis=0).astype(BF16)
    vc = jnp.concatenate([ce[:, HEAD_DIM:], co[:, HEAD_DIM:]], axis=0).astype(BF16)
    s = _dot_nt(qs, kc)
    n_idx = lax.broadcasted_iota(jnp.int32, (1, 2 * ns), 1)
    end_pos = jnp.where(n_idx < ns, n_idx * NSA_SEL + NSA_CMP - 1, (n_idx - ns) * NSA_SEL + NSA_SEL - 1)
    ok = end_pos <= pos
    s = jnp.where(ok, s, NEG)
    e = jnp.exp(s - jnp.max(s, axis=-1, keepdims=True))
    p = e / jnp.sum(e, axis=-1, keepdims=True) * jnp.where(ok, 1.0, 0.0)
    o_cmp = _dot(p.astype(BF16), vc)

    ph = p[0:tq]
    for h in range(1, N_HEADS):
        ph = ph + p[h * tq:(h + 1) * tq]
    imp = ph[:, :ns] + ph[:, ns:]
    blk = lax.broadcasted_iota(jnp.int32, (tq, ns), 1)
    cur = pos1 // NSA_SEL
    imp = jnp.where(blk > cur, NEG, imp)
    for forced in (0, cur, cur - 1):
        imp = jnp.where(blk == forced, FORCE, imp)
    sel = _topk_mask(imp, NSA_TOPK).astype(BF16)
    expand = (lax.broadcasted_iota(jnp.int32, (ns, seq), 1) // NSA_SEL
              == lax.broadcasted_iota(jnp.int32, (ns, seq), 0)).astype(BF16)
    selx_scr[...] = _dot(sel, expand)

    t_pos = q0 + lax.broadcasted_iota(jnp.int32, (tq, tk), 0)
    l_off = lax.broadcasted_iota(jnp.int32, (tq, tk), 1)

    def make_sel_body(causal):
        def sel_body(j, carry):
            k0 = pl.multiple_of(j * tk, tk)
            kv = sel_ref[pl.ds(k0, tk), :]
            s = _dot_nt(qrs, kv[:, :HEAD_DIM].astype(BF16))
            ok = selx_scr[:, pl.ds(k0, tk)] > 0.5
            if causal:
                ok = ok & ((k0 + l_off) <= t_pos)
            return _online_update(jnp.where(_tile_rows(ok), s, NEG), kv[:, HEAD_DIM:].astype(BF16), *carry)
        return sel_body

    nfull = q0 // tk
    nkb = (q0 + tq - 1) // tk + 1
    carry = lax.fori_loop(0, nfull, make_sel_body(False), _flash_init(rows, HEAD_DIM))
    _, l_sel, a_sel = lax.fori_loop(nfull, nkb, make_sel_body(True), carry)

    def win_body(j, carry):
        k0 = pl.multiple_of(j * tk, tk)
        kv = win_ref[pl.ds(k0, tk), :]
        s = _dot_nt(qrs, kv[:, :HEAD_DIM].astype(BF16))
        dist = t_pos - (k0 + l_off)
        ok = _tile_rows((dist >= 0) & (dist <= NSA_WINDOW))
        return _online_update(jnp.where(ok, s, NEG), kv[:, HEAD_DIM:].astype(BF16), *carry)

    first = jnp.maximum(q0 - NSA_WINDOW, 0) // tk
    _, l_win, a_win = lax.fori_loop(first, nkb, win_body, _flash_init(rows, HEAD_DIM))

    g = g_ref[...]
    gate = [jnp.concatenate([g[:, 3 * h + c:3 * h + c + 1] for h in range(N_HEADS)], axis=0) for c in range(3)]
    o = gate[0] * o_cmp + gate[1] * (a_sel / l_sel) + gate[2] * (a_win / l_win)
    o_ref[...] = _unstack_heads(o, tq).astype(o_ref.dtype)


def _nsa_prompt(q, qr, gates, cmp_e, cmp_o, kv_sel, kv_win, batch, seq, tq, tk):
    nq = seq // tq
    ns = seq // NSA_SEL

    def qspec(w):
        return pl.BlockSpec((tq, w), lambda b, i: (b * nq + i, 0))

    def bspec(r, w):
        return pl.BlockSpec((r, w), lambda b, i: (b, 0))

    return pl.pallas_call(
        functools.partial(_nsa_kernel, tq=tq, tk=tk),
        out_shape=jax.ShapeDtypeStruct((batch * seq, MIX_W), BF16),
        grid=(batch, nq),
        in_specs=[qspec(MIX_W), qspec(MIX_W), qspec(LANES), bspec(ns, LANES), bspec(ns, LANES),
                  bspec(seq, LANES), bspec(seq, LANES)],
        out_specs=qspec(MIX_W),
        scratch_shapes=[pltpu.VMEM((tq, seq), F32)],
        compiler_params=_cparams(("parallel", "parallel")),
        name="nsa_prompt",
    )(q, qr, gates, cmp_e, cmp_o, kv_sel, kv_win)


def _page_dma(cache_ref, pt_ref, b, base, dst_fn, sem, n_pages, start):
    def body(j, c):
        cp = pltpu.make_async_copy(cache_ref.at[pt_ref[b, j] + base], dst_fn(j), sem)
        if start:
            cp.start()
        else:
            cp.wait()
        return c

    lax.fori_loop(0, n_pages, body, 0)


def _prefetch_schedule(step, n_steps, issue):
    slot = step % 2

    @pl.when(step == 0)
    def _():
        issue(step, slot, True)

    @pl.when(step + 1 < n_steps)
    def _():
        issue(step + 1, 1 - slot, True)

    issue(step, slot, False)
    return slot


def _softmax_parts(parts):
    m = parts[0].max(axis=-1, keepdims=True)
    for s in parts[1:]:
        m = jnp.maximum(m, s.max(axis=-1, keepdims=True))
    es = [jnp.exp(s - m) for s in parts]
    l = es[0].sum(axis=-1, keepdims=True)
    for e in es[1:]:
        l = l + e.sum(axis=-1, keepdims=True)
    return es, l


def _rowdot(q, k_row):
    return jnp.sum(q.astype(F32) * k_row.astype(BF16).astype(F32), axis=-1, keepdims=True)


def _nsa_s1_kernel(pt_ref, cache_ref, q_ref, new_ref, w_ref, imp_ref, o_ref, buf, sem, *, base, past):
    b = pl.program_id(0)
    n_pages = past // PAGE_SIZE
    ns = past // NSA_SEL

    def issue(step, slot, start):
        _page_dma(cache_ref, pt_ref, step, base,
                  lambda j: buf.at[slot, pl.ds(pl.multiple_of(j * PAGE_SIZE, PAGE_SIZE), PAGE_SIZE)],
                  sem.at[slot], n_pages, start)

    slot = _prefetch_schedule(b, pl.num_programs(0), issue)

    acc_e = jnp.zeros((ns, LANES), F32)
    acc_o = jnp.zeros((ns, LANES), F32)
    for j in range(NSA_CMP):
        wj = w_ref[j:j + 1, :]
        acc_e = acc_e + buf[slot, pl.ds(j, ns, stride=NSA_SEL), :] * wj
        acc_o = acc_o + buf[slot, pl.ds(NSA_CMP + j, ns, stride=NSA_SEL), :] * wj
    ce = (acc_e * (1.0 / NSA_CMP)).astype(BF16)
    co = (acc_o * (1.0 / NSA_CMP)).astype(BF16)
    te = new_ref[0] * w_ref[0:1, :] * (1.0 / NSA_CMP)
    to = jnp.zeros_like(te)

    q = q_ref[0]
    blk = lax.broadcasted_iota(jnp.int32, (1, ns), 1)
    ok_e = blk * NSA_SEL + NSA_CMP - 1 <= past
    ok_o = blk * NSA_SEL + NSA_SEL - 1 <= past
    ok_te = ns * NSA_SEL + NSA_CMP - 1 <= past
    ok_to = ns * NSA_SEL + NSA_SEL - 1 <= past
    s_e = jnp.where(ok_e, _dot_nt(q, ce), NEG)
    s_o = jnp.where(ok_o, _dot_nt(q, co), NEG)
    s_te = jnp.where(ok_te, _rowdot(q, te), NEG)
    s_to = jnp.where(ok_to, _rowdot(q, to), NEG)
    (e_e, e_o, e_te, e_to), l = _softmax_parts([s_e, s_o, s_te, s_to])
    p_e = e_e / l * jnp.where(ok_e, 1.0, 0.0)
    p_o = e_o / l * jnp.where(ok_o, 1.0, 0.0)
    p_te = e_te / l * (1.0 if ok_te else 0.0)
    p_to = e_to / l * (1.0 if ok_to else 0.0)
    o = _dot(p_e.astype(BF16), ce) + _dot(p_o.astype(BF16), co) + p_te * te + p_to * to
    o_ref[0] = o[:, HEAD_DIM:]
    imp_past = jnp.sum(p_e + p_o, axis=0, keepdims=True)
    imp_tail = jnp.sum(p_te + p_to, axis=0, keepdims=True)
    imp_ref[0] = jnp.concatenate([imp_past, jnp.broadcast_to(imp_tail, (1, LANES))], axis=1)


def _nsa_sample_cmp(page_table, cache, q8, kv_new, w, base, past):
    nb = q8.shape[0]
    ns = past // NSA_SEL
    grid_spec = pltpu.PrefetchScalarGridSpec(
        num_scalar_prefetch=1,
        grid=(nb,),
        in_specs=[
            pl.BlockSpec(memory_space=pl.ANY),
            pl.BlockSpec((1, N_HEADS, LANES), lambda b, pt: (b, 0, 0)),
            pl.BlockSpec((1, 1, LANES), lambda b, pt: (b, 0, 0)),
            pl.BlockSpec(w.shape, lambda b, pt: (0, 0)),
        ],
        out_specs=(pl.BlockSpec((1, 1, ns + LANES), lambda b, pt: (b, 0, 0)),
                   pl.BlockSpec((1, N_HEADS, HEAD_DIM), lambda b, pt: (b, 0, 0))),
        scratch_shapes=[pltpu.VMEM((2, past, LANES), F32), pltpu.SemaphoreType.DMA((2,))],
    )
    return pl.pallas_call(
        functools.partial(_nsa_s1_kernel, base=base, past=past),
        out_shape=(jax.ShapeDtypeStruct((nb, 1, ns + LANES), F32),
                   jax.ShapeDtypeStruct((nb, N_HEADS, HEAD_DIM), F32)),
        grid_spec=grid_spec,
        compiler_params=_cparams(("arbitrary",)),
        name="nsa_sample_cmp",
    )(page_table, cache, q8, kv_new, w)


def _topk_kernel(imp_ref, idx_ref, *, cur, k):
    x = imp_ref[...]
    lane = lax.broadcasted_iota(jnp.int32, x.shape, 1)
    x = jnp.where((lane == 0) | (lane == cur) | (lane == cur - 1), FORCE, jnp.where(lane > cur, NEG, x))
    lane_f = lane.astype(F32)
    out_lane = lax.broadcasted_iota(jnp.int32, idx_ref.shape, 1)
    out = jnp.zeros(idx_ref.shape, F32)
    for i in range(k):
        m = jnp.max(x, axis=-1, keepdims=True)
        pick = jnp.min(jnp.where(x == m, lane_f, 1e9), axis=-1, keepdims=True)
        out = jnp.where(out_lane == i, pick, out)
        x = jnp.where(lane_f == pick, -3.0e38, x)
    idx_ref[...] = out.astype(jnp.int32)


def _nsa_sample_topk(imp, cur):
    nb, w = imp.shape
    return pl.pallas_call(
        functools.partial(_topk_kernel, cur=cur, k=min(NSA_TOPK, cur + 1)),
        out_shape=jax.ShapeDtypeStruct((nb, LANES), jnp.int32),
        in_specs=[pl.BlockSpec((nb, w), lambda: (0, 0))],
        out_specs=pl.BlockSpec((nb, LANES), lambda: (0, 0)),
        name="nsa_sample_topk",
    )(imp)


def _nsa_s2_kernel(pt_ref, idx_ref, cache_ref, qr_ref, seln_ref, winn_ref, winb_ref, g_ref, ocmp_ref, o_ref,
                   buf, sem, *, base, past):
    b = pl.program_id(0)
    n_past = past // NSA_SEL
    per_page = PAGE_SIZE // NSA_SEL
    nk = NSA_TOPK * NSA_SEL

    def issue(step, slot, start):
        for k in range(NSA_TOPK):
            n = jnp.minimum(idx_ref[step, k], n_past - 1)
            page = pt_ref[step, n // per_page] + base
            row = pl.multiple_of((n % per_page) * NSA_SEL, NSA_SEL)
            cp = pltpu.make_async_copy(cache_ref.at[page, pl.ds(row, NSA_SEL)],
                                       buf.at[slot, pl.ds(k * NSA_SEL, NSA_SEL)], sem.at[slot])
            if start:
                cp.start()
            else:
                cp.wait()

    slot = _prefetch_schedule(b, pl.num_programs(0), issue)

    qr = qr_ref[0]
    lane = lax.broadcasted_iota(jnp.int32, (1, nk), 1)
    tok = jnp.zeros((1, nk), jnp.int32)
    is_past = jnp.zeros((1, nk), jnp.int32)
    tail_ok = jnp.int32(0)
    for k in range(NSA_TOPK):
        ik = idx_ref[b, k]
        in_k = lane // NSA_SEL == k
        tok = jnp.where(in_k, ik * NSA_SEL + lane % NSA_SEL, tok)
        is_past = jnp.where(in_k, (ik < n_past).astype(jnp.int32), is_past)
        tail_ok = tail_ok | ((ik >= n_past) & (ik * NSA_SEL <= past)).astype(jnp.int32)
    kv = buf[slot].astype(BF16)
    s_sel = jnp.where((is_past > 0) & (tok <= past), _dot_nt(qr, kv), NEG)
    s_new = jnp.where(tail_ok > 0, _rowdot(qr, seln_ref[0]), NEG)
    (e_sel, e_new), l = _softmax_parts([s_sel, s_new])
    o_sel = (_dot(e_sel.astype(BF16), kv) + e_new * seln_ref[0]) / l

    wb = winb_ref.shape[1]
    kvw = winb_ref[0].astype(BF16)
    dist = wb - lax.broadcasted_iota(jnp.int32, (1, wb), 1)
    s_win = jnp.where((dist >= 0) & (dist <= NSA_WINDOW) & (past - dist >= 0), _dot_nt(qr, kvw), NEG)
    s_wn = _rowdot(qr, winn_ref[0])
    (e_win, e_wn), lw = _softmax_parts([s_win, s_wn])
    o_win = (_dot(e_win.astype(BF16), kvw) + e_wn * winn_ref[0]) / lw

    g = g_ref[0]
    o = g[:, 0:1] * ocmp_ref[0] + g[:, 1:2] * o_sel[:, HEAD_DIM:] + g[:, 2:3] * o_win[:, HEAD_DIM:]
    o_ref[0] = o.astype(o_ref.dtype)


def _nsa_sample_attend(page_table, idx, cache, qr8, sel_new, win_new, win_buf, gates8, o_cmp, base, win_base, past):
    nb = qr8.shape[0]
    wb = win_buf.shape[1]

    def per_b(r, w):
        return pl.BlockSpec((1, r, w), lambda b, pt, ix: (b, 0, 0))

    grid_spec = pltpu.PrefetchScalarGridSpec(
        num_scalar_prefetch=2,
        grid=(nb,),
        in_specs=[
            pl.BlockSpec(memory_space=pl.ANY),
            per_b(N_HEADS, LANES), per_b(1, LANES), per_b(1, LANES),
            pl.BlockSpec((1, wb, LANES), lambda b, pt, ix: (win_base + b, 0, 0)),
            per_b(N_HEADS, LANES), per_b(N_HEADS, HEAD_DIM),
        ],
        out_specs=per_b(N_HEADS, HEAD_DIM),
        scratch_shapes=[pltpu.VMEM((2, NSA_TOPK * NSA_SEL, LANES), F32), pltpu.SemaphoreType.DMA((2,))],
    )
    return pl.pallas_call(
        functools.partial(_nsa_s2_kernel, base=base, past=past),
        out_shape=jax.ShapeDtypeStruct((nb, N_HEADS, HEAD_DIM), BF16),
        grid_spec=grid_spec,
        compiler_params=_cparams(("arbitrary",)),
        name="nsa_sample_attend",
    )(page_table, idx, cache, qr8, sel_new, win_new, win_buf, gates8, o_cmp)


def _mla_s_kernel(pt_ref, cache_ref, q_ref, new_ref, wuv_ref, o_ref, buf, sem, m_scr, l_scr, acc_scr,
                  *, base, past, pch):
    b = pl.program_id(0)
    c = pl.program_id(1)
    nch = pl.num_programs(1)
    step = b * nch + c
    rows = pch * PAGE_SIZE
    width = cache_ref.shape[2]

    def issue(st, slot, start):
        sb = st // nch
        sc = st % nch

        def body(j, carry):
            page = pt_ref[sb, sc * pch + j] + base
            dst = buf.at[slot, pl.ds(pl.multiple_of(j * PAGE_SIZE, PAGE_SIZE), PAGE_SIZE)]
            cp = pltpu.make_async_copy(cache_ref.at[page], dst, sem.at[slot])
            if start:
                cp.start()
            else:
                cp.wait()
            return carry

        lax.fori_loop(0, pch, body, 0)

    slot = _prefetch_schedule(step, pl.num_programs(0) * nch, issue)

    @pl.when(c == 0)
    def _():
        m_scr[...] = jnp.full_like(m_scr, NEG)
        l_scr[...] = jnp.zeros_like(l_scr)
        acc_scr[...] = jnp.zeros_like(acc_scr)

    q = q_ref[0][:, :width]
    lat = buf[slot].astype(BF16)
    k_pos = c * rows + lax.broadcasted_iota(jnp.int32, (1, rows), 1)
    s = jnp.where(k_pos <= past, _dot_nt(q, lat), NEG)
    m, l, acc = _online_update(s, lat, m_scr[...], l_scr[...], acc_scr[...])
    m_scr[...] = m
    l_scr[...] = l
    acc_scr[...] = acc

    @pl.when(c == nch - 1)
    def _():
        new = new_ref[0][:, :width]
        s_new = _rowdot(q, new)
        m2 = jnp.maximum(m, s_new)
        alpha = jnp.exp(m - m2)
        e_new = jnp.exp(s_new - m2)
        o_lat = ((alpha * acc + e_new * new) / (alpha * l + e_new))[:, :MLA_DC].astype(BF16)
        full = _dot(o_lat, wuv_ref[...])
        head = lax.broadcasted_iota(jnp.int32, (N_HEADS, HEAD_DIM), 0)
        o = jnp.zeros((N_HEADS, HEAD_DIM), F32)
        for h in range(N_HEADS):
            o = jnp.where(head == h, full[:, h * HEAD_DIM:(h + 1) * HEAD_DIM], o)
        o_ref[0] = o.astype(o_ref.dtype)


def _mla_sample(page_table, cache, qc8, lat_new, wuv, base, past, pch):
    nb = qc8.shape[0]
    n_pages = past // PAGE_SIZE
    grid_spec = pltpu.PrefetchScalarGridSpec(
        num_scalar_prefetch=1,
        grid=(nb, n_pages // pch),
        in_specs=[
            pl.BlockSpec(memory_space=pl.ANY),
            pl.BlockSpec((1, N_HEADS, MLA_QW), lambda b, c, pt: (b, 0, 0)),
            pl.BlockSpec((1, 1, MLA_QW), lambda b, c, pt: (b, 0, 0)),
            pl.BlockSpec(wuv.shape, lambda b, c, pt: (0, 0)),
        ],
        out_specs=pl.BlockSpec((1, N_HEADS, HEAD_DIM), lambda b, c, pt: (b, 0, 0)),
        scratch_shapes=[pltpu.VMEM((2, pch * PAGE_SIZE, cache.shape[2]), F32), pltpu.SemaphoreType.DMA((2,)),
                        pltpu.VMEM((N_HEADS, 1), F32), pltpu.VMEM((N_HEADS, 1), F32),
                        pltpu.VMEM((N_HEADS, cache.shape[2]), F32)],
    )
    return pl.pallas_call(
        functools.partial(_mla_s_kernel, base=base, past=past, pch=pch),
        out_shape=jax.ShapeDtypeStruct((nb, N_HEADS, HEAD_DIM), BF16),
        grid_spec=grid_spec,
        compiler_params=_cparams(("arbitrary", "arbitrary")),
        name="mla_sample",
    )(page_table, cache, qc8, lat_new, wuv)


def _dot_split(x, w, parts):
    pieces = _split3(x)[:parts]
    out = _dot(pieces[0], w)
    for p in pieces[1:]:
        out = out + _dot(p, w)
    return out


def _fox_bias_kernel(pt_ref, cache_ref, u_ref, v_ref, e_ref, o_ref, buf, sem, *, base, past, group):
    i = pl.program_id(0)
    n_pages = past // PAGE_SIZE

    def issue(step, slot, start):
        for g in range(group):
            _page_dma(cache_ref, pt_ref, step * group + g, base,
                      lambda j, g=g: buf.at[slot, pl.ds(g * n_pages + j, 1)],
                      sem.at[slot], n_pages, start)

    slot = _prefetch_schedule(i, pl.num_programs(0), issue)
    x = buf[slot]
    within = _dot_split(x, u_ref[...], 3)
    tot = _dot_split(x, v_ref[...], 3)
    upper = (lax.broadcasted_iota(jnp.int32, (n_pages, n_pages), 1)
             > lax.broadcasted_iota(jnp.int32, (n_pages, n_pages), 0)).astype(BF16)
    later = jnp.concatenate(
        [_dot_split_lhs(upper, tot[g * n_pages:(g + 1) * n_pages]) for g in range(group)], axis=0)
    o_ref[...] = within + _dot_split(later, e_ref[...], 3)


def _dot_split_lhs(w, x):
    hi, mid, lo = _split3(x)
    return (_dot(w, hi) + _dot(w, mid)) + _dot(w, lo)


def _fox_sample_bias(page_table, cache, u, v, e, base, past, group):
    nb = page_table.shape[0]
    n_pages = past // PAGE_SIZE
    pw = cache.shape[2]
    grid_spec = pltpu.PrefetchScalarGridSpec(
        num_scalar_prefetch=1,
        grid=(nb // group,),
        in_specs=[
            pl.BlockSpec(memory_space=pl.ANY),
            pl.BlockSpec(u.shape, lambda i, pt: (0, 0)),
            pl.BlockSpec(v.shape, lambda i, pt: (0, 0)),
            pl.BlockSpec(e.shape, lambda i, pt: (0, 0)),
        ],
        out_specs=pl.BlockSpec((group * n_pages, pw), lambda i, pt: (i, 0)),
        scratch_shapes=[pltpu.VMEM((2, group * n_pages, pw), F32), pltpu.SemaphoreType.DMA((2,))],
    )
    return pl.pallas_call(
        functools.partial(_fox_bias_kernel, base=base, past=past, group=group),
        out_shape=jax.ShapeDtypeStruct((nb * n_pages, pw), F32),
        grid_spec=grid_spec,
        compiler_params=_cparams(("arbitrary",)),
        name="fox_sample_bias",
    )(page_table, cache, u, v, e)


def _fox_s_kernel(pt_ref, cache_ref, q_ref, bias_ref, new_ref, lnew_ref, o_ref, buf, sem, *, base, past):
    b = pl.program_id(0)
    n_pages = past // PAGE_SIZE

    def issue(step, slot, start):
        _page_dma(cache_ref, pt_ref, step, base,
                  lambda j: buf.at[slot, pl.ds(pl.multiple_of(j * PAGE_SIZE, PAGE_SIZE), PAGE_SIZE)],
                  sem.at[slot], n_pages, start)

    slot = _prefetch_schedule(b, pl.num_programs(0), issue)
    q = (q_ref[0] * ATT_SCALE).astype(BF16)
    kv = buf[slot].astype(BF16)
    bias =jnp.concatenate([bias_ref[j] for j in range(n_pages)], axis=1)
    k_pos = lax.broadcasted_iota(jnp.int32, (1, past), 1)
    s = jnp.where(k_pos <= past, _dot_nt(q, kv) + bias + lnew_ref[0][:, 0:1], NEG)
    s_new = _rowdot(q, new_ref[0])
    (e, e_new), l = _softmax_parts([s, s_new])
    o = (_dot(e.astype(BF16), kv) + e_new * new_ref[0]) / l
    o_ref[0] = o[:, HEAD_DIM:].astype(o_ref.dtype)


def _fox_sample(page_table, cache, q8, bias, kv_new, lnew, base, past):
    nb = q8.shape[0]
    n_pages = past // PAGE_SIZE

    def per_b(r, w):
        return pl.BlockSpec((1, r, w), lambda b, pt: (b, 0, 0))

    grid_spec = pltpu.PrefetchScalarGridSpec(
        num_scalar_prefetch=1,
        grid=(nb,),
        in_specs=[
            pl.BlockSpec(memory_space=pl.ANY),
            per_b(N_HEADS, LANES),
            pl.BlockSpec((n_pages, N_HEADS, PAGE_SIZE), lambda b, pt: (b, 0, 0)),
            per_b(1, LANES), per_b(N_HEADS, LANES),
        ],
        out_specs=per_b(N_HEADS, HEAD_DIM),
        scratch_shapes=[pltpu.VMEM((2, past, LANES), F32), pltpu.SemaphoreType.DMA((2,))],
    )
    return pl.pallas_call(
        functools.partial(_fox_s_kernel, base=base, past=past),
        out_shape=jax.ShapeDtypeStruct((nb, N_HEADS, HEAD_DIM), BF16),
        grid_spec=grid_spec,
        compiler_params=_cparams(("arbitrary",)),
        name="fox_sample",
    )(page_table, cache, q8, bias, kv_new, lnew)


def _gmlp_s_kernel(u_ref, v_ref, w_ref, b_ref, o_ref):
    o_ref[...] = (u_ref[...] * (v_ref[...] * w_ref[...] + b_ref[...])).astype(o_ref.dtype)


def _gmlp_sample(u, v, w00, b0, row0, nb):
    blk = row0 // nb
    return pl.pallas_call(
        _gmlp_s_kernel,
        out_shape=jax.ShapeDtypeStruct((nb, MIX_W), BF16),
        grid=(1,),
        in_specs=[pl.BlockSpec((nb, MIX_W), lambda i: (blk, 0)), pl.BlockSpec((nb, MIX_W), lambda i: (blk, 0)),
                  pl.BlockSpec((1, MIX_W), lambda i: (0, 0)), pl.BlockSpec((1, MIX_W), lambda i: (0, 0))],
        out_specs=pl.BlockSpec((nb, MIX_W), lambda i: (0, 0)),
        name="gmlp_sample",
    )(u, v, w00, b0)


def _lane_page_dma(cache_ref, pt_ref, b, base, buf, slot, sem, n_pages, start, page0=0):
    def body(j, c):
        dst = buf.at[slot, :, pl.ds(pl.multiple_of(j * PAGE_SIZE, PAGE_SIZE), PAGE_SIZE)]
        cp = pltpu.make_async_copy(cache_ref.at[pt_ref[b, page0 + j] + base], dst, sem)
        if start:
            cp.start()
        else:
            cp.wait()
        return c

    lax.fori_loop(0, n_pages, body, 0)


def _split2(x):
    hi = x.astype(BF16)
    return hi, (x - hi.astype(F32)).astype(BF16)


def _nsa_t1_kernel(pt_ref, cache_ref, q_ref, new_ref, w_ref, wt_ref, pool_ref, imp_ref, o_ref, buf, sem,
                   *, base, past):
    b = pl.program_id(0)
    n_pages = past // PAGE_SIZE
    gt, nbg = pool_ref.shape
    hb = nbg // 2

    def issue(step, slot, start):
        _lane_page_dma(cache_ref, pt_ref, step, base, buf, slot, sem.at[slot], n_pages, start)

    slot = _prefetch_schedule(b, pl.num_programs(0), issue)

    wt = jnp.concatenate([wt_ref[...]] * (gt // LANES), axis=1)
    pool = pool_ref[...]
    groups = []
    for g in range(past // gt):
        hi, mid = _split2(buf[slot, :, g * gt:(g + 1) * gt] * wt)
        groups.append((_dot(hi, pool) + _dot(mid, pool)) * (1.0 / NSA_CMP))
    cmp_t = jnp.concatenate(groups, axis=1).astype(BF16)
    te = new_ref[0] * w_ref[0:1, :] * (1.0 / NSA_CMP)
    to = jnp.zeros_like(te)

    q = q_ref[0]
    lane = lax.broadcasted_iota(jnp.int32, (1, past // NSA_CMP), 1)
    blk = (lane // nbg) * nbg + 2 * (lane % hb) + (lane % nbg) // hb
    ok = blk * NSA_CMP + NSA_CMP - 1 <= past
    ns = past // NSA_SEL
    ok_te = ns * NSA_SEL + NSA_CMP - 1 <= past
    ok_to = ns * NSA_SEL + NSA_SEL - 1 <= past
    s = jnp.where(ok, _dot(q, cmp_t), NEG)
    s_te = jnp.where(ok_te, _rowdot(q, te), NEG)
    s_to = jnp.where(ok_to, _rowdot(q, to), NEG)
    (e, e_te, e_to), l = _softmax_parts([s, s_te, s_to])
    p = e / l * jnp.where(ok, 1.0, 0.0)
    p_te = e_te / l * (1.0 if ok_te else 0.0)
    p_to = e_to / l * (1.0 if ok_to else 0.0)
    o = _dot_nt(p.astype(BF16), cmp_t) + p_te * te + p_to * to
    o_ref[0] = o[:, HEAD_DIM:]
    ph = jnp.sum(p, axis=0, keepdims=True)
    imp = [ph[:, g * nbg:g * nbg + hb] + ph[:, g * nbg + hb:(g + 1) * nbg] for g in range(past // gt)]
    imp_tail = jnp.sum(p_te + p_to, axis=0, keepdims=True)
    imp_ref[0] = jnp.concatenate(imp + [jnp.broadcast_to(imp_tail, (1, LANES))], axis=1)


def _nsa_sample_cmp_t(page_table, cache, q8, kv_new, w, wt, pool, base, past):
    nb = q8.shape[0]
    ns = past // NSA_SEL
    grid_spec = pltpu.PrefetchScalarGridSpec(
        num_scalar_prefetch=1,
        grid=(nb,),
        in_specs=[
            pl.BlockSpec(memory_space=pl.ANY),
            pl.BlockSpec((1, N_HEADS, LANES), lambda b, pt: (b, 0, 0)),
            pl.BlockSpec((1, 1, LANES), lambda b, pt: (b, 0, 0)),
            pl.BlockSpec(w.shape, lambda b, pt: (0, 0)),
            pl.BlockSpec(wt.shape, lambda b, pt: (0, 0)),
            pl.BlockSpec(pool.shape, lambda b, pt: (0, 0)),
        ],
        out_specs=(pl.BlockSpec((1, 1, ns + LANES), lambda b, pt: (b, 0, 0)),
                   pl.BlockSpec((1, N_HEADS, HEAD_DIM), lambda b, pt: (b, 0, 0))),
        scratch_shapes=[pltpu.VMEM((2, LANES, past), F32), pltpu.SemaphoreType.DMA((2,))],
    )
    return pl.pallas_call(
        functools.partial(_nsa_t1_kernel, base=base, past=past),
        out_shape=(jax.ShapeDtypeStruct((nb, 1, ns + LANES), F32),
                   jax.ShapeDtypeStruct((nb, N_HEADS, HEAD_DIM), F32)),
        grid_spec=grid_spec,
        compiler_params=_cparams(("arbitrary",)),
        name="nsa_sample_cmp",
    )(page_table, cache, q8, kv_new, w, wt, pool)


def _nsa_t2_kernel(pt_ref, idx_ref, cache_ref, qr_ref, seln_ref, winn_ref, winb_ref, g_ref, ocmp_ref, o_ref,
                   buf, sem, *, base, past):
    b = pl.program_id(0)
    n_past = past // NSA_SEL
    per_page = PAGE_SIZE // NSA_SEL
    nk = NSA_TOPK * PAGE_SIZE

    def issue(step, slot, start):
        for k in range(NSA_TOPK):
            n = jnp.minimum(idx_ref[step, k], n_past - 1)
            page = pt_ref[step, n // per_page] + base
            cp = pltpu.make_async_copy(cache_ref.at[page], buf.at[slot, :, pl.ds(k * PAGE_SIZE, PAGE_SIZE)],
                                       sem.at[slot])
            if start:
                cp.start()
            else:
                cp.wait()

    slot = _prefetch_schedule(b, pl.num_programs(0), issue)

    qr = qr_ref[0]
    lane = lax.broadcasted_iota(jnp.int32, (1, nk), 1)
    in_page = lane % PAGE_SIZE
    tok = jnp.zeros((1, nk), jnp.int32)
    valid = jnp.zeros((1, nk), jnp.int32)
    tail_ok = jnp.int32(0)
    for k in range(NSA_TOPK):
        ik = idx_ref[b, k]
        n = jnp.minimum(ik, n_past - 1)
        in_k = lane // PAGE_SIZE == k
        tok = jnp.where(in_k, (n // per_page) * PAGE_SIZE + in_page, tok)
        mine = jnp.where(in_page // NSA_SEL == n % per_page, (ik < n_past).astype(jnp.int32), 0)
        valid = jnp.where(in_k, mine, valid)
        tail_ok = tail_ok | ((ik >= n_past) & (ik * NSA_SEL <= past)).astype(jnp.int32)
    kv = buf[slot].astype(BF16)
    s_sel = jnp.where(jnp.where(tok <= past, valid, 0) > 0, _dot(qr, kv), NEG)
    s_new = jnp.where(tail_ok > 0, _rowdot(qr, seln_ref[0]), NEG)
    (e_sel, e_new), l = _softmax_parts([s_sel, s_new])
    o_sel = (_dot_nt(e_sel.astype(BF16), kv) + e_new * seln_ref[0]) / l

    wb = winb_ref.shape[2]
    kvw = winb_ref[0].astype(BF16)
    dist = wb - lax.broadcasted_iota(jnp.int32, (1, wb), 1)
    in_win = jnp.where(dist <= NSA_WINDOW, jnp.where(past - dist >= 0, 1, 0), 0)
    s_win = jnp.where(in_win > 0, _dot(qr, kvw), NEG)
    s_wn = _rowdot(qr, winn_ref[0])
    (e_win, e_wn), lw = _softmax_parts([s_win, s_wn])
    o_win = (_dot_nt(e_win.astype(BF16), kvw) + e_wn * winn_ref[0]) / lw

    g = g_ref[0]
    o = g[:, 0:1] * ocmp_ref[0] + g[:, 1:2] * o_sel[:, HEAD_DIM:] + g[:, 2:3] * o_win[:, HEAD_DIM:]
    o_ref[0] = o.astype(o_ref.dtype)


def _nsa_sample_attend_t(page_table, idx, cache, qr8, sel_new, win_new, win_t, gates8, o_cmp, base, win_base, past):
    nb = qr8.shape[0]
    wb = win_t.shape[2]

    def per_b(r, w):
        return pl.BlockSpec((1, r, w), lambda b, pt, ix: (b, 0, 0))

    grid_spec = pltpu.PrefetchScalarGridSpec(
        num_scalar_prefetch=2,
        grid=(nb,),
        in_specs=[
            pl.BlockSpec(memory_space=pl.ANY),
            per_b(N_HEADS, LANES), per_b(1, LANES), per_b(1, LANES),
            pl.BlockSpec((1, LANES, wb), lambda b, pt, ix: (win_base + b, 0, 0)),
            per_b(N_HEADS, LANES), per_b(N_HEADS, HEAD_DIM),
        ],
        out_specs=per_b(N_HEADS, HEAD_DIM),
        scratch_shapes=[pltpu.VMEM((2, LANES, NSA_TOPK * PAGE_SIZE), F32), pltpu.SemaphoreType.DMA((2,))],
    )
    return pl.pallas_call(
        functools.partial(_nsa_t2_kernel, base=base, past=past),
        out_shape=jax.ShapeDtypeStruct((nb, N_HEADS, HEAD_DIM), BF16),
        grid_spec=grid_spec,
        compiler_params=_cparams(("arbitrary",)),
        name="nsa_sample_attend",
    )(page_table, idx, cache, qr8, sel_new, win_new, win_t, gates8, o_cmp)


def _mla_t_kernel(pt_ref, cache_ref, q_ref, new_ref, wuv_ref, o_ref, buf, sem, m_scr, l_scr, acc_scr,
                  *, base, past, pch):
    b = pl.program_id(0)
    c = pl.program_id(1)
    nch = pl.num_programs(1)
    step = b * nch + c
    cols = pch * PAGE_SIZE
    width = cache_ref.shape[1]

    def issue(st, slot, start):
        _lane_page_dma(cache_ref, pt_ref, st // nch, base, buf, slot, sem.at[slot], pch, start,
                       page0=(st % nch) * pch)

    slot = _prefetch_schedule(step, pl.num_programs(0) * nch, issue)

    @pl.when(c == 0)
    def _():
        m_scr[...] = jnp.full_like(m_scr, NEG)
        l_scr[...] = jnp.zeros_like(l_scr)
        acc_scr[...] = jnp.zeros_like(acc_scr)

    q = q_ref[0]
    lat_t = buf[slot].astype(BF16)
    k_pos = c * cols + lax.broadcasted_iota(jnp.int32, (1, cols), 1)
    s = jnp.where(k_pos <= past, _dot(q[:, :width], lat_t), NEG)
    m_old = m_scr[...]
    m = jnp.maximum(m_old, jnp.max(s, axis=-1, keepdims=True))
    alpha = jnp.exp(m_old - m)
    p = jnp.exp(s - m)
    l = alpha * l_scr[...] + jnp.sum(p, axis=-1, keepdims=True)
    acc = alpha * acc_scr[...] + _dot_nt(p.astype(BF16), lat_t[:MLA_DC])
    m_scr[...] = m
    l_scr[...] = l
    acc_scr[...] = acc

    @pl.when(c == nch - 1)
    def _():
        new = new_ref[0]
        s_new = _rowdot(q, new)
        m2 = jnp.maximum(m, s_new)
        a2 = jnp.exp(m - m2)
        e_new = jnp.exp(s_new - m2)
        o_lat = ((a2 * acc + e_new * new[:, :MLA_DC]) / (a2 * l + e_new)).astype(BF16)
        full = _dot(o_lat, wuv_ref[...])
        head = lax.broadcasted_iota(jnp.int32, (N_HEADS, HEAD_DIM), 0)
        o = jnp.zeros((N_HEADS, HEAD_DIM), F32)
        for h in range(N_HEADS):
            o = jnp.where(head == h, full[:, h * HEAD_DIM:(h + 1) * HEAD_DIM], o)
        o_ref[0] = o.astype(o_ref.dtype)


def _mla_sample_t(page_table, cache, qc8, lat_new, wuv, base, past, pch):
    nb = qc8.shape[0]
    n_pages = past // PAGE_SIZE
    width = cache.shape[1]
    grid_spec = pltpu.PrefetchScalarGridSpec(
        num_scalar_prefetch=1,
        grid=(nb, n_pages // pch),
        in_specs=[
            pl.BlockSpec(memory_space=pl.ANY),
            pl.BlockSpec((1, N_HEADS, MLA_QW), lambda b, c, pt: (b, 0, 0)),
            pl.BlockSpec((1, 1, MLA_QW), lambda b, c, pt: (b, 0, 0)),
            pl.BlockSpec(wuv.shape, lambda b, c, pt: (0, 0)),
        ],
        out_specs=pl.BlockSpec((1, N_HEADS, HEAD_DIM), lambda b, c, pt: (b, 0, 0)),
        scratch_shapes=[pltpu.VMEM((2, width, pch * PAGE_SIZE), F32), pltpu.SemaphoreType.DMA((2,)),
                        pltpu.VMEM((N_HEADS, 1), F32), pltpu.VMEM((N_HEADS, 1), F32),
                        pltpu.VMEM((N_HEADS, MLA_DC), F32)],
    )
    return pl.pallas_call(
        functools.partial(_mla_t_kernel, base=base, past=past, pch=pch),
        out_shape=jax.ShapeDtypeStruct((nb, N_HEADS, HEAD_DIM), BF16),
        grid_spec=grid_spec,
        compiler_params=_cparams(("arbitrary", "arbitrary")),
        name="mla_sample",
    )(page_table, cache, qc8, lat_new, wuv)


def _fox_t_kernel(pt_ref, cache_ref, lcache_ref, q_ref, new_ref, lnew_ref, u_ref, o_ref,
                  buf, lbuf, sem, lsem, *, base, past):
    b = pl.program_id(0)
    n_pages = past // PAGE_SIZE

    def issue(step, slot, start):
        _lane_page_dma(cache_ref, pt_ref, step, base, buf, slot, sem.at[slot], n_pages, start)
        _page_dma(lcache_ref, pt_ref, step, base,
                  lambda j: lbuf.at[slot, pl.ds(pl.multiple_of(j * N_HEADS, N_HEADS), N_HEADS)],
                  lsem.at[slot], n_pages, start)

    slot = _prefetch_schedule(b, pl.num_programs(0), issue)

    x = lbuf[slot]
    within = _dot_split(x, u_ref[...], 3)
    totals = jnp.sum(x, axis=-1, keepdims=True)
    later = jnp.zeros((N_HEADS, 1), F32)
    pieces = [None] * n_pages
    for j in reversed(range(n_pages)):
        pieces[j] = within[j * N_HEADS:(j + 1) * N_HEADS] + later
        later = later + totals[j * N_HEADS:(j + 1) * N_HEADS]
    bias = jnp.concatenate(pieces, axis=1)

    q = (q_ref[0] * ATT_SCALE).astype(BF16)
    kv = buf[slot].astype(BF16)
    k_pos = lax.broadcasted_iota(jnp.int32, (1, past), 1)
    s = jnp.where(k_pos <= past, _dot(q, kv) + bias + lnew_ref[0][:, 0:1], NEG)
    s_new = _rowdot(q, new_ref[0])
    (e, e_new), l = _softmax_parts([s, s_new])
    o = (_dot_nt(e.astype(BF16), kv) + e_new * new_ref[0]) / l
    o_ref[0] = o[:, HEAD_DIM:].astype(o_ref.dtype)


def _fox_sample_t(page_table, cache, lcache, q8, kv_new, lnew, u, base, past):
    nb = q8.shape[0]
    n_pages = past // PAGE_SIZE

    def per_b(r, w):
        return pl.BlockSpec((1, r, w), lambda b, pt: (b, 0, 0))

    def const(a):
        return pl.BlockSpec(a.shape, lambda b, pt: (0, 0))

    grid_spec = pltpu.PrefetchScalarGridSpec(
        num_scalar_prefetch=1,
        grid=(nb,),
        in_specs=[
            pl.BlockSpec(memory_space=pl.ANY), pl.BlockSpec(memory_space=pl.ANY),
            per_b(N_HEADS, LANES), per_b(1, LANES), per_b(N_HEADS, LANES),
            const(u),
        ],
        out_specs=per_b(N_HEADS, HEAD_DIM),
        scratch_shapes=[pltpu.VMEM((2, LANES, past), F32), pltpu.VMEM((2, n_pages * N_HEADS, PAGE_SIZE), F32),
                        pltpu.SemaphoreType.DMA((2,)), pltpu.SemaphoreType.DMA((2,))],
    )
    return pl.pallas_call(
        functools.partial(_fox_t_kernel, base=base, past=past),
        out_shape=jax.ShapeDtypeStruct((nb, N_HEADS, HEAD_DIM), BF16),
        grid_spec=grid_spec,
        compiler_params=_cparams(("arbitrary",)),
        name="fox_sample",
    )(page_table, cache, lcache, q8, kv_new, lnew, u)


def _kv_pages_t(cache):
    l, p, t = cache.shape[:3]
    return jnp.transpose(cache, (0, 1, 3, 4, 5, 2)).reshape(l * p, -1, t)


def _pages_t(cache):
    l, p, t, c = cache.shape
    return jnp.transpose(cache, (0, 1, 3, 2)).reshape(l * p, c, t)


def _sample_consts(past):
    n_pages = past // PAGE_SIZE
    gt = min(past, 32 * PAGE_SIZE)
    nbg = gt // NSA_CMP
    n = np.arange(gt) // NSA_CMP
    col = (n % 2) * (nbg // 2) + n // 2
    pool = col[:, None] == np.arange(nbg)[None, :]
    t = np.arange(PAGE_SIZE)
    u = t[:, None] > t[None, :]
    return tuple(jnp.asarray(a, dtype=BF16) for a in (pool, u))


def _swap_rot(w, rot):
    half = rot // 2
    return jnp.concatenate([-w[..., half:rot], w[..., :half], jnp.zeros_like(w[..., rot:])], axis=-1)


def _pad_last(w, n):
    return jnp.pad(w, [(0, 0)] * (w.ndim - 1) + [(0, n - w.shape[-1])])


def _small_proj_weight(w_in):
    d = w_in.shape[0]
    off = [0]

    def take(n):
        s = w_in[:, off[0]:off[0] + n]
        off[0] += n
        return s

    nsa_q = take(MIX_W)
    nsa_kv = take(6 * HEAD_DIM)
    nsa_g = take(3 * N_HEADS)
    gm_u, gm_v, mla_cq = take(MIX_W), take(MIX_W), take(MLA_DQ)
    mla_ckv, mla_kr = take(MLA_DC), take(MLA_DR)
    fox_q, fox_kv, fox_f = take(MIX_W), take(2 * HEAD_DIM), take(N_HEADS)
    nsa_qs = _swap_rot(nsa_q.reshape(d, N_HEADS, HEAD_DIM), ROT_DIM).reshape(d, MIX_W)
    sel_ks = _swap_rot(nsa_kv[:, 2 * HEAD_DIM:3 * HEAD_DIM], ROT_DIM)
    win_ks = _swap_rot(nsa_kv[:, 4 * HEAD_DIM:5 * HEAD_DIM], ROT_DIM)
    segs = [nsa_q, nsa_qs, gm_u, gm_v, mla_cq, fox_q,
            nsa_kv[:, :LANES], nsa_kv[:, LANES:2 * LANES], nsa_kv[:, 2 * LANES:],
            _pad_last(sel_ks, LANES), _pad_last(win_ks, LANES), _pad_last(nsa_g, LANES),
            mla_ckv, _pad_last(mla_kr, LANES), _pad_last(_swap_rot(mla_kr, MLA_DR), LANES),
            fox_kv, _pad_last(fox_f, LANES)]
    w_small = jnp.concatenate(segs, axis=1).astype(BF16)
    assert w_small.shape[1] == Z_W
    return w_small, w_in[:, off[0]:].astype(BF16)


def _block_diag(w):
    h, r, c = w.shape
    eye = jnp.eye(h, dtype=bool)
    return jnp.where(eye[:, None, :, None], w[:, :, None, :], 0).reshape(h * r, h * c)


def _mla_weights(wq, wuk, wuv):
    dq = wq.shape[0]
    wq = wq.reshape(dq, N_HEADS, HEAD_DIM + MLA_DR)
    rope = wq[:, :, HEAD_DIM:]
    lead = jnp.zeros((dq, N_HEADS, MLA_DC), wq.dtype)
    wa = _pad_last(jnp.concatenate([lead, rope], axis=-1), MLA_QW).reshape(dq, N_HEADS * MLA_QW)
    wb = _pad_last(jnp.concatenate([lead, _swap_rot(rope, MLA_DR)], axis=-1), MLA_QW).reshape(dq, N_HEADS * MLA_QW)
    wqn = wq[:, :, :HEAD_DIM].reshape(dq, MIX_W)
    wbd = _block_diag(_pad_last(jnp.transpose(wuk, (1, 2, 0)), MLA_QW))
    wuv_bd = _block_diag(jnp.transpose(wuv, (1, 0, 2)))
    return (wqn.astype(BF16), wbd.astype(BF16), wa.astype(BF16), wb.astype(BF16), wuv_bd.astype(BF16),
            wuv.reshape(MLA_DC, MIX_W).astype(BF16))


def _rope_tables(pos):
    def cs(half, theta):
        inv = theta ** (-jnp.arange(half, dtype=F32) / half)
        ang = pos.astype(F32)[:, None] * inv[None, :]
        return jnp.cos(ang), jnp.sin(ang)

    n = pos.shape[0]
    c, s = cs(ROT_DIM // 2, ROPE_THETA)
    one, zero = jnp.ones((n, HEAD_DIM - ROT_DIM), F32), jnp.zeros((n, HEAD_DIM - ROT_DIM), F32)
    hc = jnp.concatenate([c, c, one], axis=1)
    hs = jnp.concatenate([s, s, zero], axis=1)
    tab_nsa = jnp.concatenate([hc, hc, hs, hs, hc, jnp.ones((n, HEAD_DIM), F32), hs, jnp.zeros((n, HEAD_DIM), F32)],
                              axis=1)
    c, s = cs(MLA_DR // 2, MLA_THETA)
    tail = jnp.zeros((n, MLA_QW - MLA_DC - MLA_DR), F32)
    tab_mla = jnp.concatenate([jnp.ones((n, MLA_DC), F32), c, c, tail, jnp.zeros((n, MLA_DC), F32), s, s, tail], axis=1)
    return tab_nsa, tab_mla


def _fox_bias_consts():
    t1, h1 = np.divmod(np.arange(PAGE_SIZE * N_HEADS), N_HEADS)
    h2, t2 = np.divmod(np.arange(PAGE_SIZE * N_HEADS), PAGE_SIZE)
    u = (h1[:, None] == h2[None, :]) & (t1[:, None] > t2[None, :])
    copies = LANES // N_HEADS
    hv, rv = np.divmod(np.arange(LANES), copies)
    v = h1[:, None] == hv[None, :]
    e = (hv[:, None] == h2[None, :]) & (rv[:, None] == 0)
    return tuple(jnp.asarray(a, dtype=BF16) for a in (u, v, e))


def kernel(x_prompt, x_sample, cache_nsa_cmp, cache_nsa_sel, state_nsa_win, cache_mla, cache_fox_kv, cache_fox_logf, page_table, p_prompt, p_sample, n_ffn1, ffn1_in, ffn1_out, n_mix, w_in, nsa_cmp_w, gm_norm, gm_ws, gm_bs, mla_q_norm, mla_wq, mla_kv_norm, mla_wuk, mla_wuv, fox_bf, w_br, w_o, n_ffn2, ffn2_in, ffn2_out, n_ple, ple_wg, ple_wp, norm_final):
    batch, seq, d = x_prompt.shape
    nb = x_sample.shape[0]
    depth = w_in.shape[0]
    n_pool = cache_mla.shape[1]
    past = page_table.shape[1] * PAGE_SIZE
    wb = state_nsa_win.shape[2]
    n_p = batch * seq
    n = n_p + nb
    tm = 384 if n % 384 == 0 else 128
    tq, tk = 128, 256

    x = jnp.concatenate([x_prompt.reshape(n_p, d), x_sample.reshape(nb, d)], axis=0)
    pos = jnp.concatenate([jnp.arange(n_p, dtype=jnp.int32) % seq, jnp.full((nb,), past, jnp.int32)])
    tab_nsa, tab_mla = _rope_tables(pos)
    pool_m, bias_u = _sample_consts(past)
    c_nsa_cmp = _kv_pages_t(cache_nsa_cmp)
    c_nsa_sel = _kv_pages_t(cache_nsa_sel)
    c_mla = _pages_t(cache_mla)
    c_fox_kv = _kv_pages_t(cache_fox_kv)
    c_fox_logf = _pages_t(cache_fox_logf)
    win_t = jnp.transpose(state_nsa_win, (0, 1, 3, 4, 5, 2))
    win_state = win_t.reshape(depth * nb, LANES, wb)

    st_p, st_s = [], []
    for l in range(depth):
        w_small, w_gate = _small_proj_weight(w_in[l])
        wqn, wbd, wa, wb_, wuv_bd, wuv_flat = _mla_weights(mla_wq[l], mla_wuk[l], mla_wuv[l])
        cmp_w = nsa_cmp_w[l].reshape(NSA_CMP, LANES)
        bs_exp = jnp.repeat(gm_bs[l].T, HEAD_DIM, axis=1)
        w00 = jnp.repeat(gm_ws[l][:, 0, 0], HEAD_DIM).reshape(1, MIX_W)
        b0 = jnp.repeat(gm_bs[l][:, 0], HEAD_DIM).reshape(1, MIX_W)

        x = _ffn(x, n_ffn1[l], ffn1_in[l].astype(BF16), ffn1_out[l].astype(BF16), tm, 1024)
        z = _proj(x, n_mix[l], w_small, tm)
        q, qr, kv_sel, kv_win, gates, gm_u, gm_v, qc, lat, logf = _prep(
            z, tab_nsa, tab_mla, gm_norm[l], mla_q_norm[l], mla_kv_norm[l],
            _pad_last(fox_bf[l].reshape(1, N_HEADS), LANES), wqn, wbd, wa, wb_, tm)
        kv_cmp = z[:, Z_NSA_CMP:Z_NSA_CMP + LANES]
        fox_kv = z[:, Z_FOX_KV:Z_FOX_KV + LANES]

        cmp_e, cmp_o = _nsa_compress(kv_cmp.reshape(n // NSA_SEL, NSA_SEL * LANES), cmp_w,
                                     n_p // NSA_SEL, seq // NSA_SEL)
        o_nsa = _nsa_prompt(q, qr, gates, cmp_e, cmp_o, kv_sel, kv_win, batch, seq, tq, tk)
        o_gm = _gmlp_prompt(gm_u, gm_v, gm_ws[l], bs_exp, n_p, 512)
        o_mla = _mla_prompt(qc, lat, wuv_bd, batch, seq, tq, tk)
        c_col, c_row = _fox_cumsum(logf, batch, seq)
        o_fox = _fox_prompt(z, c_col, c_row, batch, seq, tq, tk)

        base = l * n_pool
        q8 = _pad_last(q[n_p:].reshape(nb, N_HEADS, HEAD_DIM), LANES)
        qr8 = _pad_last(qr[n_p:].reshape(nb, N_HEADS, HEAD_DIM), LANES)
        cmp_wt = jnp.tile(cmp_w.T, (1, LANES // NSA_CMP))
        imp, o_cmp = _nsa_sample_cmp_t(page_table, c_nsa_cmp, q8, kv_cmp[n_p:].reshape(nb, 1, LANES), cmp_w, cmp_wt,
                                       pool_m, base, past)
        idx = _nsa_sample_topk(imp.reshape(nb, -1), past // NSA_SEL)
        gates8 = _pad_last(gates[n_p:, :3 * N_HEADS].reshape(nb, N_HEADS, 3), LANES)
        o_nsa_s = _nsa_sample_attend_t(page_table, idx, c_nsa_sel, qr8, kv_sel[n_p:].reshape(nb, 1, LANES),
                                       kv_win[n_p:].reshape(nb, 1, LANES), win_state, gates8, o_cmp,
                                       base, l * nb, past)
        o_gm_s = _gmlp_sample(gm_u, gm_v, w00, b0, n_p, nb)
        o_mla_s = _mla_sample_t(page_table, c_mla, qc[n_p:].reshape(nb, N_HEADS, MLA_QW),
                                lat[n_p:].reshape(nb, 1, MLA_QW), wuv_flat, base, past, min(64, past // PAGE_SIZE))
        fq8 = _pad_last(z[n_p:, Z_FOX_Q:Z_FOX_Q + MIX_W].reshape(nb, N_HEADS, HEAD_DIM), LANES)
        lnew = jnp.broadcast_to(logf[n_p:, :N_HEADS, None], (nb, N_HEADS, LANES))
        o_fox_s = _fox_sample_t(page_table, c_fox_kv, c_fox_logf, fq8, fox_kv[n_p:].reshape(nb, 1, LANES), lnew,
                                bias_u, base, past)

        outs = [jnp.concatenate([a, b.reshape(nb, MIX_W)], axis=0)
                for a, b in ((o_nsa, o_nsa_s), (o_gm, o_gm_s), (o_mla, o_mla_s), (o_fox, o_fox_s))]
        x = _merge(x, n_mix[l], outs, w_gate, w_br[l].astype(BF16), w_o[l].astype(BF16), tm)
        x = _ffn(x, n_ffn2[l], ffn2_in[l].astype(BF16), ffn2_out[l].astype(BF16), tm, 1024)
        p_all = jnp.concatenate([p_prompt[l].reshape(n_p, -1), p_sample[l].reshape(nb, -1)], axis=0)
        x = _ple(x, n_ple[l], p_all, ple_wg[l].astype(BF16), ple_wp[l].astype(BF16), norm_final,
                 l == depth - 1, tm)

        def kv5(a, rows):
            return a.reshape(rows, -1, 2, 1, HEAD_DIM)

        win_p = kv5(kv_win[:n_p], batch)
        st_p.append((kv5(kv_cmp[:n_p], batch), kv5(kv_sel[:n_p], batch), win_p[:, seq - min(NSA_WINDOW, seq):],
                     lat[:n_p, :MLA_DC + MLA_DR].reshape(batch, seq, -1), kv5(fox_kv[:n_p], batch),
                     logf[:n_p, :N_HEADS].reshape(batch, seq, N_HEADS)))
        new_t = kv_win[n_p:].reshape(nb, 2, 1, HEAD_DIM, 1)
        win_s = jnp.transpose(jnp.concatenate([win_t[l][..., 1:], new_t], axis=-1), (0, 4, 1, 2, 3))
        st_s.append((kv5(kv_cmp[n_p:], nb), kv5(kv_sel[n_p:], nb), win_s,
                     lat[n_p:, :MLA_DC + MLA_DR].reshape(nb, 1, -1), kv5(fox_kv[n_p:], nb),
                     logf[n_p:, :N_HEADS].reshape(nb, 1, N_HEADS), gm_v[n_p:].reshape(nb, 1, N_HEADS, HEAD_DIM)))

    y = x
    outs_p = [jnp.stack(a) for a in zip(*st_p)]
    outs_s = [jnp.stack(a) for a in zip(*st_s)]
    return (y[:n_p].reshape(batch, seq, d), y[n_p:].reshape(nb, 1, d), *outs_p, *outs_s)
```

```python
import functools

import numpy as np
import jax
import jax.numpy as jnp
from jax import lax
from jax.experimental import pallas as pl
from jax.experimental.pallas import tpu as pltpu

D_MODEL = 2048
N_HEADS = 8
HEAD_DIM = 64
MIX_W = N_HEADS * HEAD_DIM
N_BRANCH = 4
D_FF = 2 * D_MODEL
PLE_DIM = 256
ROPE_THETA = 500000.0
ROT_DIM = HEAD_DIM // 4
NSA_CMP = 32
NSA_SEL = 64
NSA_TOPK = 16
NSA_WINDOW = 512
GM_CHUNK = 128
MLA_DQ = D_MODEL // 4
MLA_DC = D_MODEL // 16
MLA_DR = HEAD_DIM // 2
MLA_THETA = 10000.0
EPS = 1e-6
NEG = -1e30
FORCE = 1e4
ATT_SCALE = HEAD_DIM ** -0.5
MLA_SCALE = (HEAD_DIM + MLA_DR) ** -0.5
PAGE_SIZE = 128

F32 = jnp.float32
BF16 = jnp.bfloat16
LANES = 128
MLA_QW = 2 * LANES
VMEM_LIMIT = 56 * 1024 * 1024

Z_NSA_Q, Z_NSA_QS, Z_GM_U, Z_GM_V, Z_MLA_CQ, Z_FOX_Q = (i * MIX_W for i in range(6))
Z_NSA_CMP = 6 * MIX_W
Z_NSA_SEL = Z_NSA_CMP + LANES
Z_NSA_WIN = Z_NSA_SEL + LANES
Z_NSA_KSS = Z_NSA_WIN + LANES
Z_NSA_KSW = Z_NSA_KSS + LANES
Z_NSA_G = Z_NSA_KSW + LANES
Z_MLA_CKV = Z_NSA_G + LANES
Z_MLA_KR = Z_MLA_CKV + LANES
Z_MLA_KRS = Z_MLA_KR + LANES
Z_FOX_KV = Z_MLA_KRS + LANES
Z_FOX_F = Z_FOX_KV + LANES
Z_W = Z_FOX_F + LANES


def _cparams(sem):
    return pltpu.CompilerParams(dimension_semantics=sem, vmem_limit_bytes=VMEM_LIMIT)


def _const_spec(shape):
    nd = len(shape)
    return pl.BlockSpec(shape, lambda *_: (0,) * nd, pipeline_mode=pl.Buffered(1))


def _rms(x, g):
    return x * lax.rsqrt(jnp.mean(x * x, axis=-1, keepdims=True) + EPS) * g


def _dot(a, b):
    return jnp.dot(a, b, preferred_element_type=F32)


def _dot_nt(a, b):
    return lax.dot_general(a, b, (((1,), (1,)), ((), ())), preferred_element_type=F32)


def _ffn_kernel(x_ref, g_ref, wg_ref, wu_ref, wo_ref, o_ref, h_scr, acc_scr):
    k = pl.program_id(1)

    @pl.when(k == 0)
    def _():
        h_scr[...] = _rms(x_ref[...], g_ref[...]).astype(BF16)
        acc_scr[...] = jnp.zeros_like(acc_scr)

    h = h_scr[...]
    g = _dot(h, wg_ref[...])
    u = _dot(h, wu_ref[...])
    a = (g * jax.nn.sigmoid(g)) * u
    acc_scr[...] += _dot(a.astype(BF16), wo_ref[...])

    @pl.when(k == pl.num_programs(1) - 1)
    def _():
        o_ref[...] = x_ref[...] + 0.5 * acc_scr[...]


def _ffn(x, g, w_in, w_out, tm, tc):
    n, d = x.shape
    dff = w_out.shape[0]
    nc = dff // tc
    return pl.pallas_call(
        _ffn_kernel,
        out_shape=jax.ShapeDtypeStruct((n, d), F32),
        grid=(n // tm, nc),
        in_specs=[
            pl.BlockSpec((tm, d), lambda i, k: (i, 0)),
            pl.BlockSpec((1, d), lambda i, k: (0, 0)),
            pl.BlockSpec((d, tc), lambda i, k: (0, k)),
            pl.BlockSpec((d, tc), lambda i, k: (0, k + nc)),
            pl.BlockSpec((tc, d), lambda i, k: (k, 0)),
        ],
        out_specs=pl.BlockSpec((tm, d), lambda i, k: (i, 0)),
        scratch_shapes=[pltpu.VMEM((tm, d), BF16), pltpu.VMEM((tm, d), F32)],
        compiler_params=_cparams(("parallel", "arbitrary")),
        name="ffn",
    )(x, g.reshape(1, d), w_in, w_in, w_out)


def _proj_kernel(x_ref, g_ref, w_ref, o_ref):
    h = _rms(x_ref[...], g_ref[...]).astype(BF16)
    o_ref[...] = _dot(h, w_ref[...])


def _proj(x, g, w, tm):
    n, d = x.shape
    zw = w.shape[1]
    return pl.pallas_call(
        _proj_kernel,
        out_shape=jax.ShapeDtypeStruct((n, zw), F32),
        grid=(n // tm,),
        in_specs=[
            pl.BlockSpec((tm, d), lambda i: (i, 0)),
            pl.BlockSpec((1, d), lambda i: (0, 0)),
            _const_spec((d, zw)),
        ],
        out_specs=pl.BlockSpec((tm, zw), lambda i: (i, 0)),
        compiler_params=_cparams(("parallel",)),
        name="in_proj",
    )(x, g.reshape(1, d), w)


def _merge_kernel(x_ref, g_ref, o_ref, wg_ref, wbr_ref, wo_ref, out_ref, h_scr, acc_scr):
    s = pl.program_id(1)
    half = s % 2

    @pl.when(s == 0)
    def _():
        h_scr[...] = _rms(x_ref[...], g_ref[...]).astype(BF16)
        acc_scr[...] = jnp.zeros_like(acc_scr)

    gate = jax.nn.sigmoid(_dot(h_scr[...], wg_ref[...]))
    acc_scr[half] += gate * _dot(o_ref[0], wbr_ref[0])

    @pl.when(s == pl.num_programs(1) - 1)
    def _():
        hd = wo_ref.shape[0] // 2
        y = _dot(acc_scr[0].astype(BF16), wo_ref[:hd, :])
        y += _dot(acc_scr[1].astype(BF16), wo_ref[hd:, :])
        out_ref[...] = x_ref[...] + y


def _merge(x, g, outs, w_gate, w_br, w_o, tm):
    n, d = x.shape
    hd = d // 2
    return pl.pallas_call(
        _merge_kernel,
        out_shape=jax.ShapeDtypeStruct((n, d), F32),
        grid=(n // tm, 2 * N_BRANCH),
        in_specs=[
            pl.BlockSpec((tm, d), lambda i, s: (i, 0)),
            pl.BlockSpec((1, d), lambda i, s: (0, 0)),
            pl.BlockSpec((1, tm, MIX_W), lambda i, s: (s // 2, i, 0)),
            pl.BlockSpec((d, hd), lambda i, s: (0, s)),
            pl.BlockSpec((1, MIX_W, hd), lambda i, s: (s // 2, 0, s % 2)),
            _const_spec((d, d)),
        ],
        out_specs=pl.BlockSpec((tm, d), lambda i, s: (i, 0)),
        scratch_shapes=[pltpu.VMEM((tm, d), BF16), pltpu.VMEM((2, tm, hd), F32)],
        compiler_params=_cparams(("parallel", "arbitrary")),
        name="merge",
    )(x, g.reshape(1, d), outs, w_gate, w_br, w_o)


def _ple_kernel(x_ref, g_ref, p_ref, wg_ref, wp_ref, gf_ref, o_ref, *, final):
    x = x_ref[...]
    gate = jax.nn.sigmoid(_dot(_rms(x, g_ref[...]).astype(BF16), wg_ref[...]))
    y = x + gate * _dot(p_ref[...].astype(BF16), wp_ref[...])
    if final:
        y = _rms(y, gf_ref[...])
    o_ref[...] = y


def _ple(x, g, p, w_g, w_p, g_final, final, tm):
    n, d = x.shape
    pd = p.shape[1]
    return pl.pallas_call(
        functools.partial(_ple_kernel, final=final),
        out_shape=jax.ShapeDtypeStruct((n, d), F32),
        grid=(n // tm,),
        in_specs=[
            pl.BlockSpec((tm, d), lambda i: (i, 0)),
            pl.BlockSpec((1, d), lambda i: (0, 0)),
            pl.BlockSpec((tm, pd), lambda i: (i, 0)),
            _const_spec((d, d)),
            _const_spec((pd, d)),
            pl.BlockSpec((1, d), lambda i: (0, 0)),
        ],
        out_specs=pl.BlockSpec((tm, d), lambda i: (i, 0)),
        compiler_params=_cparams(("parallel",)),
        name="ple",
    )(x, g.reshape(1, d), p, w_g, w_p, g_final.reshape(1, d))


def _gelu(x):
    return 0.5 * x * (1.0 + lax.erf(x * np.float32(np.sqrt(0.5))))


def _log_sigmoid(x):
    return jnp.minimum(x, 0.0) - jnp.log1p(jnp.exp(-jnp.abs(x)))


def _prep_kernel(zq_ref, zqs_ref, zu_ref, zv_ref, zcq_ref, zsel_ref, zwin_ref, zkss_ref, zksw_ref,
                 zg_ref, zckv_ref, zkr_ref, zkrs_ref, zf_ref, tn_ref, tm_ref,
                 gmn_ref, qn_ref, kvn_ref, bf_ref, wqn_ref, wbd_ref, wa_ref, wb_ref,
                 oq_ref, oqr_ref, osel_ref, owin_ref, ogate_ref, ou_ref, ov_ref, oqc_ref,
                 olat_ref, ologf_ref):
    tn = tn_ref[...]
    cos_q = jnp.concatenate([tn[:, 0:LANES]] * (MIX_W // LANES), axis=1)
    sin_q = jnp.concatenate([tn[:, LANES:2 * LANES]] * (MIX_W // LANES), axis=1)
    cos_k = tn[:, 2 * LANES:3 * LANES]
    sin_k = tn[:, 3 * LANES:4 * LANES]
    q = zq_ref[...]
    oq_ref[...] = (q * ATT_SCALE).astype(BF16)
    oqr_ref[...] = ((q * cos_q + zqs_ref[...] * sin_q) * ATT_SCALE).astype(BF16)
    osel_ref[...] = zsel_ref[...] * cos_k + zkss_ref[...] * sin_k
    owin_ref[...] = zwin_ref[...] * cos_k + zksw_ref[...] * sin_k
    ogate_ref[...] = jax.nn.sigmoid(zg_ref[...])

    ou_ref[...] = _gelu(zu_ref[...])
    gv = _gelu(zv_ref[...])
    gc = gv - jnp.mean(gv, axis=-1, keepdims=True)
    ov_ref[...] = gc * lax.rsqrt(jnp.mean(gc * gc, axis=-1, keepdims=True) + EPS) * gmn_ref[...]

    tm = tm_ref[...]
    cqn = _rms(zcq_ref[...], qn_ref[...]).astype(BF16)
    q_nope = _dot(cqn, wqn_ref[...]).astype(BF16)
    qa = _dot(q_nope, wbd_ref[...]) + _dot(cqn, wa_ref[...])
    qb = _dot(cqn, wb_ref[...])
    cos_m = jnp.concatenate([tm[:, :MLA_QW]] * N_HEADS, axis=1)
    sin_m = jnp.concatenate([tm[:, MLA_QW:]] * N_HEADS, axis=1)
    oqc_ref[...] = ((qa * cos_m + qb * sin_m) * MLA_SCALE).astype(BF16)
    c = _rms(zckv_ref[...], kvn_ref[...])
    kr = zkr_ref[...] * tm[:, LANES:2 * LANES] + zkrs_ref[...] * tm[:, MLA_QW + LANES:]
    olat_ref[...] = jnp.concatenate([c, kr], axis=1)
    ologf_ref[...] = _log_sigmoid(zf_ref[...] + bf_ref[...])


def _prep(z, tab_nsa, tab_mla, gm_norm, q_norm, kv_norm, fox_bf, wqn, wbd, wa, wb, tm):
    n = z.shape[0]

    def zb(off, w):
        return pl.BlockSpec((tm, w), lambda i, o=off // w: (i, o))

    def row(w):
        return pl.BlockSpec((tm, w), lambda i: (i, 0))

    def vec(w):
        return pl.BlockSpec((1, w), lambda i: (0, 0))

    z_specs = [zb(Z_NSA_Q, MIX_W), zb(Z_NSA_QS, MIX_W), zb(Z_GM_U, MIX_W), zb(Z_GM_V, MIX_W),
               zb(Z_MLA_CQ, MIX_W), zb(Z_NSA_SEL, LANES), zb(Z_NSA_WIN, LANES), zb(Z_NSA_KSS, LANES),
               zb(Z_NSA_KSW, LANES), zb(Z_NSA_G, LANES), zb(Z_MLA_CKV, LANES), zb(Z_MLA_KR, LANES),
               zb(Z_MLA_KRS, LANES), zb(Z_FOX_F, LANES)]
    qcw = N_HEADS * MLA_QW
    out_shape = (
        jax.ShapeDtypeStruct((n, MIX_W), BF16), jax.ShapeDtypeStruct((n, MIX_W), BF16),
        jax.ShapeDtypeStruct((n, LANES), F32), jax.ShapeDtypeStruct((n, LANES), F32),
        jax.ShapeDtypeStruct((n, LANES), F32),
        jax.ShapeDtypeStruct((n, MIX_W), F32), jax.ShapeDtypeStruct((n, MIX_W), F32),
        jax.ShapeDtypeStruct((n, qcw), BF16), jax.ShapeDtypeStruct((n, MLA_QW), F32),
        jax.ShapeDtypeStruct((n, LANES), F32),
    )
    out_specs = (row(MIX_W), row(MIX_W), row(LANES), row(LANES), row(LANES), row(MIX_W), row(MIX_W),
                 row(qcw), row(MLA_QW), row(LANES))
    return pl.pallas_call(
        _prep_kernel,
        out_shape=out_shape,
        grid=(n // tm,),
        in_specs=z_specs + [row(4 * LANES), row(2 * MLA_QW), vec(MIX_W), vec(MLA_DQ), vec(LANES), vec(LANES),
                            _const_spec(wqn.shape), _const_spec(wbd.shape), _const_spec(wa.shape),
                            _const_spec(wb.shape)],
        out_specs=out_specs,
        compiler_params=_cparams(("parallel",)),
        name="mixer_prep",
    )(*([z] * len(z_specs)), tab_nsa, tab_mla, gm_norm.reshape(1, -1), q_norm.reshape(1, -1),
      kv_norm.reshape(1, -1), fox_bf, wqn, wbd, wa, wb)


def _stack_heads(x, width):
    return jnp.concatenate([x[:, h * width:(h + 1) * width] for h in range(N_HEADS)], axis=0)


def _unstack_heads(x, t):
    return jnp.concatenate([x[h * t:(h + 1) * t] for h in range(N_HEADS)], axis=1)


def _tile_rows(x):
    return jnp.concatenate([x] * N_HEADS, axis=0)


def _online_update(s, v, m, l, acc):
    m_new = jnp.maximum(m, jnp.max(s, axis=-1, keepdims=True))
    alpha = jnp.exp(m - m_new)
    p = jnp.exp(s - m_new)
    l = alpha * l + jnp.sum(p, axis=-1, keepdims=True)
    acc = alpha * acc + _dot(p.astype(BF16), v)
    return m_new, l, acc


def _flash_init(rows, width):
    return (jnp.full((rows, 1), NEG, F32), jnp.zeros((rows, 1), F32), jnp.zeros((rows, width), F32))


def _gmlp_kernel(u_ref, v_ref, ws_ref, bs_ref, o_ref):
    c = GM_CHUNK
    tril = (lax.broadcasted_iota(jnp.int32, (c, c), 0) >= lax.broadcasted_iota(jnp.int32, (c, c), 1)).astype(F32)
    grp = lax.broadcasted_iota(jnp.int32, (c, MIX_W), 1) // HEAD_DIM
    ws = [(ws_ref[g] * tril).astype(BF16) for g in range(N_HEADS)]
    for ch in range(u_ref.shape[0] // c):
        v = v_ref[ch * c:(ch + 1) * c, :].astype(BF16)
        z = bs_ref[...]
        for g in range(N_HEADS):
            z = z + jnp.where(grp == g, _dot(ws[g], v), 0.0)
        o_ref[ch * c:(ch + 1) * c, :] = (u_ref[ch * c:(ch + 1) * c, :] * z).astype(o_ref.dtype)


def _gmlp_prompt(u, v, ws, bs_exp, n_rows, tq):
    return pl.pallas_call(
        _gmlp_kernel,
        out_shape=jax.ShapeDtypeStruct((n_rows, MIX_W), BF16),
        grid=(n_rows // tq,),
        in_specs=[
            pl.BlockSpec((tq, MIX_W), lambda i: (i, 0)),
            pl.BlockSpec((tq, MIX_W), lambda i: (i, 0)),
            _const_spec(ws.shape),
            _const_spec(bs_exp.shape),
        ],
        out_specs=pl.BlockSpec((tq, MIX_W), lambda i: (i, 0)),
        compiler_params=_cparams(("parallel",)),
        name="gmlp_prompt",
    )(u, v, ws, bs_exp)


def _split3(x):
    hi = x.astype(BF16)
    r = x - hi.astype(F32)
    mid = r.astype(BF16)
    lo = (r - mid.astype(F32)).astype(BF16)
    return hi, mid, lo


def _cumsum_kernel(x_ref, col_ref, row_ref, *, blk):
    tri = (lax.broadcasted_iota(jnp.int32, (blk, blk), 0)
           >= lax.broadcasted_iota(jnp.int32, (blk, blk), 1)).astype(BF16)
    carry = jnp.zeros((1, x_ref.shape[1]), F32)
    for c in range(x_ref.shape[0] // blk):
        hi, mid, lo = _split3(x_ref[c * blk:(c + 1) * blk, :])
        cs = (_dot(tri, hi) + _dot(tri, mid)) + _dot(tri, lo) + carry
        col_ref[c * blk:(c + 1) * blk, :] = cs
        row_ref[:, c * blk:(c + 1) * blk] = cs.T[:row_ref.shape[0], :]
        carry = cs[blk - 1:blk, :]


def _fox_cumsum(logf, batch, seq):
    return pl.pallas_call(
        functools.partial(_cumsum_kernel, blk=256),
        out_shape=(jax.ShapeDtypeStruct((batch * seq, LANES), F32),
                   jax.ShapeDtypeStruct((batch * N_HEADS, seq), F32)),
        grid=(batch,),
        in_specs=[pl.BlockSpec((seq, LANES), lambda b: (b, 0))],
        out_specs=(pl.BlockSpec((seq, LANES), lambda b: (b, 0)),
                   pl.BlockSpec((N_HEADS, seq), lambda b: (b, 0))),
        compiler_params=_cparams(("parallel",)),
        name="fox_cumsum",
    )(logf)


def _fox_kernel(q_ref, kv_ref, cc_ref, cr_ref, o_ref, *, tq, tk):
    i = pl.program_id(1)
    q0 = i * tq
    qs = _stack_heads((q_ref[...] * ATT_SCALE).astype(BF16), HEAD_DIM)
    cc = cc_ref[...]
    cq = jnp.concatenate([cc[:, h:h + 1] for h in range(N_HEADS)], axis=0)
    t_pos = q0 + lax.broadcasted_iota(jnp.int32, (tq, tk), 0)
    l_off = lax.broadcasted_iota(jnp.int32, (tq, tk), 1)

    def body(j, carry):
        k0 = pl.multiple_of(j * tk, tk)
        kv = kv_ref[pl.ds(k0, tk), :]
        s = _dot_nt(qs, kv[:, :HEAD_DIM].astype(BF16))
        ck = cr_ref[:, pl.ds(k0, tk)]
        ok = (k0 + l_off) <= t_pos
        s = jnp.concatenate(
            [jnp.where(ok, s[h * tq:(h + 1) * tq] - ck[h:h + 1, :], NEG) for h in range(N_HEADS)], axis=0)
        return _online_update(s + cq, kv[:, HEAD_DIM:].astype(BF16), *carry)

    nkb = (q0 + tq - 1) // tk + 1
    m, l, acc = lax.fori_loop(0, nkb, body, _flash_init(N_HEADS * tq, HEAD_DIM))
    o_ref[...] = _unstack_heads(acc / l, tq).astype(o_ref.dtype)


def _fox_prompt(z, c_col, c_row, batch, seq, tq, tk):
    nq = seq // tq
    return pl.pallas_call(
        functools.partial(_fox_kernel, tq=tq, tk=tk),
        out_shape=jax.ShapeDtypeStruct((batch * seq, MIX_W), BF16),
        grid=(batch, nq),
        in_specs=[
            pl.BlockSpec((tq, MIX_W), lambda b, i: (b * nq + i, Z_FOX_Q // MIX_W)),
            pl.BlockSpec((seq, LANES), lambda b, i: (b, Z_FOX_KV // LANES)),
            pl.BlockSpec((tq, LANES), lambda b, i: (b * nq + i, 0)),
            pl.BlockSpec((N_HEADS, seq), lambda b, i: (b, 0)),
        ],
        out_specs=pl.BlockSpec((tq, MIX_W), lambda b, i: (b * nq + i, 0)),
        compiler_params=_cparams(("parallel", "parallel")),
        name="fox_prompt",
    )(z, z, c_col, c_row)


def _mla_kernel(q_ref, lat_ref, wuv_ref, o_ref, *, tq, tk):
    i = pl.program_id(1)
    q0 = i * tq
    qs = _stack_heads(q_ref[...], MLA_QW)
    t_pos = q0 + lax.broadcasted_iota(jnp.int32, (tq, tk), 0)
    l_off = lax.broadcasted_iota(jnp.int32, (tq, tk), 1)

    def body(j, carry):
        k0 = pl.multiple_of(j * tk, tk)
        lat = lat_ref[pl.ds(k0, tk), :].astype(BF16)
        s = _dot_nt(qs, lat)
        ok = _tile_rows((k0 + l_off) <= t_pos)
        return _online_update(jnp.where(ok, s, NEG), lat[:, :MLA_DC], *carry)

    nkb = (q0 + tq - 1) // tk + 1
    m, l, acc = lax.fori_loop(0, nkb, body, _flash_init(N_HEADS * tq, MLA_DC))
    o_lat = _unstack_heads(acc / l, tq).astype(BF16)
    o_ref[...] = _dot(o_lat, wuv_ref[...]).astype(o_ref.dtype)


def _mla_prompt(qc, lat, wuv_bd, batch, seq, tq, tk):
    nq = seq // tq
    return pl.pallas_call(
        functools.partial(_mla_kernel, tq=tq, tk=tk),
        out_shape=jax.ShapeDtypeStruct((batch * seq, MIX_W), BF16),
        grid=(batch, nq),
        in_specs=[
            pl.BlockSpec((tq, N_HEADS * MLA_QW), lambda b, i: (b * nq + i, 0)),
            pl.BlockSpec((seq, MLA_QW), lambda b, i: (b, 0)),
            _const_spec(wuv_bd.shape),
        ],
        out_specs=pl.BlockSpec((tq, MIX_W), lambda b, i: (b * nq + i, 0)),
        compiler_params=_cparams(("parallel", "parallel")),
        name="mla_prompt",
    )(qc, lat, wuv_bd)


def _cmp_kernel(x_ref, w_ref, e_ref, o_ref):
    half = NSA_CMP * LANES
    for part, out in ((0, e_ref), (1, o_ref)):
        acc = jnp.zeros(out.shape, F32)
        for j in range(NSA_CMP):
            lo = part * half + j * LANES
            acc = acc + x_ref[:, lo:lo + LANES] * w_ref[j:j + 1, :]
        out[...] = acc * (1.0 / NSA_CMP)


def _nsa_compress(kv_rows, w, n_rows, tr):
    return pl.pallas_call(
        _cmp_kernel,
        out_shape=(jax.ShapeDtypeStruct((n_rows, LANES), F32), jax.ShapeDtypeStruct((n_rows, LANES), F32)),
        grid=(n_rows // tr,),
        in_specs=[pl.BlockSpec((tr, NSA_SEL * LANES), lambda i: (i, 0)), _const_spec(w.shape)],
        out_specs=(pl.BlockSpec((tr, LANES), lambda i: (i, 0)), pl.BlockSpec((tr, LANES), lambda i: (i, 0))),
        compiler_params=_cparams(("parallel",)),
        name="nsa_compress",
    )(kv_rows, w)


def _topk_mask(v, k):
    n = v.shape[1]
    idx = lax.broadcasted_iota(jnp.int32, v.shape, 1)
    rank = jnp.zeros(v.shape, F32)
    for m in range(n):
        col = v[:, m:m + 1]
        rank = rank + jnp.where(idx > m, jnp.where(col >= v, 1.0, 0.0), jnp.where(col > v, 1.0, 0.0))
    return jnp.where(rank < k, 1.0, 0.0)


def _nsa_kernel(q_ref, qr_ref, g_ref, ce_ref, co_ref, sel_ref, win_ref, o_ref, selx_scr, *, tq, tk):
    i = pl.program_id(1)
    q0 = i * tq
    seq = sel_ref.shape[0]
    ns = seq // NSA_SEL
    rows = N_HEADS * tq
    qs = _stack_heads(q_ref[...], HEAD_DIM)
    qrs = _stack_heads(qr_ref[...], HEAD_DIM)
    pos1 = q0 + lax.broadcasted_iota(jnp.int32, (tq, 1), 0)
    pos = _tile_rows(pos1)

    ce = ce_ref[...]
    co = co_ref[...]
    kc = jnp.concatenate([ce[:, :HEAD_DIM], co[:, :HEAD_DIM]], axis=0).astype(BF16)
    vc = jnp.concatenate([ce[:, HEAD_DIM:], co[:, HEAD_DIM:]], axis=0).astype(BF16)
    s = _dot_nt(qs, kc)
    n_idx = lax.broadcasted_iota(jnp.int32, (1, 2 * ns), 1)
    end_pos = jnp.where(n_idx < ns, n_idx * NSA_SEL + NSA_CMP - 1, (n_idx - ns) * NSA_SEL + NSA_SEL - 1)
    ok = end_pos <= pos
    s = jnp.where(ok, s, NEG)
    e = jnp.exp(s - jnp.max(s, axis=-1, keepdims=True))
    p = e / jnp.sum(e, axis=-1, keepdims=True) * jnp.where(ok, 1.0, 0.0)
    o_cmp = _dot(p.astype(BF16), vc)

    ph = p[0:tq]
    for h in range(1, N_HEADS):
        ph = ph + p[h * tq:(h + 1) * tq]
    imp = ph[:, :ns] + ph[:, ns:]
    blk = lax.broadcasted_iota(jnp.int32, (tq, ns), 1)
    cur = pos1 // NSA_SEL
    imp = jnp.where(blk > cur, NEG, imp)
    for forced in (0, cur, cur - 1):
        imp = jnp.where(blk == forced, FORCE, imp)
    sel = _topk_mask(imp, NSA_TOPK).astype(BF16)
    expand = (lax.broadcasted_iota(jnp.int32, (ns, seq), 1) // NSA_SEL
              == lax.broadcasted_iota(jnp.int32, (ns, seq), 0)).astype(BF16)
    selx_scr[...] = _dot(sel, expand)

    t_pos = q0 + lax.broadcasted_iota(jnp.int32, (tq, tk), 0)
    l_off = lax.broadcasted_iota(jnp.int32, (tq, tk), 1)

    def sel_body(j, carry):
        k0 = pl.multiple_of(j * tk, tk)
        kv = sel_ref[pl.ds(k0, tk), :]
        s = _dot_nt(qrs, kv[:, :HEAD_DIM].astype(BF16))
        ok = _tile_rows(((k0 + l_off) <= t_pos) & (selx_scr[:, pl.ds(k0, tk)] > 0.5))
        return _online_update(jnp.where(ok, s, NEG), kv[:, HEAD_DIM:].astype(BF16), *carry)

    nkb = (q0 + tq - 1) // tk + 1
    _, l_sel, a_sel = lax.fori_loop(0, nkb, sel_body, _flash_init(rows, HEAD_DIM))

    def win_body(j, carry):
        k0 = pl.multiple_of(j * tk, tk)
        kv = win_ref[pl.ds(k0, tk), :]
        s = _dot_nt(qrs, kv[:, :HEAD_DIM].astype(BF16))
        dist = t_pos - (k0 + l_off)
        ok = _tile_rows((dist >= 0) & (dist <= NSA_WINDOW))
        return _online_update(jnp.where(ok, s, NEG), kv[:, HEAD_DIM:].astype(BF16), *carry)

    first = jnp.maximum(q0 - NSA_WINDOW, 0) // tk
    _, l_win, a_win = lax.fori_loop(first, nkb, win_body, _flash_init(rows, HEAD_DIM))

    g = g_ref[...]
    gate = [jnp.concatenate([g[:, 3 * h + c:3 * h + c + 1] for h in range(N_HEADS)], axis=0) for c in range(3)]
    o = gate[0] * o_cmp + gate[1] * (a_sel / l_sel) + gate[2] * (a_win / l_win)
    o_ref[...] = _unstack_heads(o, tq).astype(o_ref.dtype)


def _nsa_prompt(q, qr, gates, cmp_e, cmp_o, kv_sel, kv_win, batch, seq, tq, tk):
    nq = seq // tq
    ns = seq // NSA_SEL

    def qspec(w):
        return pl.BlockSpec((tq, w), lambda b, i: (b * nq + i, 0))

    def bspec(r, w):
        return pl.BlockSpec((r, w), lambda b, i: (b, 0))

    return pl.pallas_call(
        functools.partial(_nsa_kernel, tq=tq, tk=tk),
        out_shape=jax.ShapeDtypeStruct((batch * seq, MIX_W), BF16),
        grid=(batch, nq),
        in_specs=[qspec(MIX_W), qspec(MIX_W), qspec(LANES), bspec(ns, LANES), bspec(ns, LANES),
                  bspec(seq, LANES), bspec(seq, LANES)],
        out_specs=qspec(MIX_W),
        scratch_shapes=[pltpu.VMEM((tq, seq), F32)],
        compiler_params=_cparams(("parallel", "parallel")),
        name="nsa_prompt",
    )(q, qr, gates, cmp_e, cmp_o, kv_sel, kv_win)


def _page_dma(cache_ref, pt_ref, b, base, dst_fn, sem, n_pages, start):
    def body(j, c):
        cp = pltpu.make_async_copy(cache_ref.at[pt_ref[b, j] + base], dst_fn(j), sem)
        if start:
            cp.start()
        else:
            cp.wait()
        return c

    lax.fori_loop(0, n_pages, body, 0)


def _prefetch_schedule(step, n_steps, issue):
    slot = step % 2

    @pl.when(step == 0)
    def _():
        issue(step, slot, True)

    @pl.when(step + 1 < n_steps)
    def _():
        issue(step + 1, 1 - slot, True)

    issue(step, slot, False)
    return slot


def _softmax_parts(parts):
    m = parts[0].max(axis=-1, keepdims=True)
    for s in parts[1:]:
        m = jnp.maximum(m, s.max(axis=-1, keepdims=True))
    es = [jnp.exp(s - m) for s in parts]
    l = es[0].sum(axis=-1, keepdims=True)
    for e in es[1:]:
        l = l + e.sum(axis=-1, keepdims=True)
    return es, l


def _rowdot(q, k_row):
    return jnp.sum(q.astype(F32) * k_row.astype(BF16).astype(F32), axis=-1, keepdims=True)


def _nsa_s1_kernel(pt_ref, cache_ref, q_ref, new_ref, w_ref, imp_ref, o_ref, buf, sem, *, base, past):
    b = pl.program_id(0)
    n_pages = past // PAGE_SIZE
    ns = past // NSA_SEL

    def issue(step, slot, start):
        _page_dma(cache_ref, pt_ref, step, base,
                  lambda j: buf.at[slot, pl.ds(pl.multiple_of(j * PAGE_SIZE, PAGE_SIZE), PAGE_SIZE)],
                  sem.at[slot], n_pages, start)

    slot = _prefetch_schedule(b, pl.num_programs(0), issue)

    acc_e = jnp.zeros((ns, LANES), F32)
    acc_o = jnp.zeros((ns, LANES), F32)
    for j in range(NSA_CMP):
        wj = w_ref[j:j + 1, :]
        acc_e = acc_e + buf[slot, pl.ds(j, ns, stride=NSA_SEL), :] * wj
        acc_o = acc_o + buf[slot, pl.ds(NSA_CMP + j, ns, stride=NSA_SEL), :] * wj
    ce = (acc_e * (1.0 / NSA_CMP)).astype(BF16)
    co = (acc_o * (1.0 / NSA_CMP)).astype(BF16)
    te = new_ref[0] * w_ref[0:1, :] * (1.0 / NSA_CMP)
    to = jnp.zeros_like(te)

    q = q_ref[0]
    blk = lax.broadcasted_iota(jnp.int32, (1, ns), 1)
    ok_e = blk * NSA_SEL + NSA_CMP - 1 <= past
    ok_o = blk * NSA_SEL + NSA_SEL - 1 <= past
    ok_te = ns * NSA_SEL + NSA_CMP - 1 <= past
    ok_to = ns * NSA_SEL + NSA_SEL - 1 <= past
    s_e = jnp.where(ok_e, _dot_nt(q, ce), NEG)
    s_o = jnp.where(ok_o, _dot_nt(q, co), NEG)
    s_te = jnp.where(ok_te, _rowdot(q, te), NEG)
    s_to = jnp.where(ok_to, _rowdot(q, to), NEG)
    (e_e, e_o, e_te, e_to), l = _softmax_parts([s_e, s_o, s_te, s_to])
    p_e = e_e / l * jnp.where(ok_e, 1.0, 0.0)
    p_o = e_o / l * jnp.where(ok_o, 1.0, 0.0)
    p_te = e_te / l * (1.0 if ok_te else 0.0)
    p_to = e_to / l * (1.0 if ok_to else 0.0)
    o = _dot(p_e.astype(BF16), ce) + _dot(p_o.astype(BF16), co) + p_te * te + p_to * to
    o_ref[0] = o[:, HEAD_DIM:]
    imp_past = jnp.sum(p_e + p_o, axis=0, keepdims=True)
    imp_tail = jnp.sum(p_te + p_to, axis=0, keepdims=True)
    imp_ref[0] = jnp.concatenate([imp_past, jnp.broadcast_to(imp_tail, (1, LANES))], axis=1)


def _nsa_sample_cmp(page_table, cache, q8, kv_new, w, base, past):
    nb = q8.shape[0]
    ns = past // NSA_SEL
    grid_spec = pltpu.PrefetchScalarGridSpec(
        num_scalar_prefetch=1,
        grid=(nb,),
        in_specs=[
            pl.BlockSpec(memory_space=pl.ANY),
            pl.BlockSpec((1, N_HEADS, LANES), lambda b, pt: (b, 0, 0)),
            pl.BlockSpec((1, 1, LANES), lambda b, pt: (b, 0, 0)),
            pl.BlockSpec(w.shape, lambda b, pt: (0, 0)),
        ],
        out_specs=(pl.BlockSpec((1, 1, ns + LANES), lambda b, pt: (b, 0, 0)),
                   pl.BlockSpec((1, N_HEADS, HEAD_DIM), lambda b, pt: (b, 0, 0))),
        scratch_shapes=[pltpu.VMEM((2, past, LANES), F32), pltpu.SemaphoreType.DMA((2,))],
    )
    return pl.pallas_call(
        functools.partial(_nsa_s1_kernel, base=base, past=past),
        out_shape=(jax.ShapeDtypeStruct((nb, 1, ns + LANES), F32),
                   jax.ShapeDtypeStruct((nb, N_HEADS, HEAD_DIM), F32)),
        grid_spec=grid_spec,
        compiler_params=_cparams(("arbitrary",)),
        name="nsa_sample_cmp",
    )(page_table, cache, q8, kv_new, w)


def _topk_kernel(imp_ref, idx_ref, *, cur, k):
    x = imp_ref[...]
    lane = lax.broadcasted_iota(jnp.int32, x.shape, 1)
    x = jnp.where((lane == 0) | (lane == cur) | (lane == cur - 1), FORCE, jnp.where(lane > cur, NEG, x))
    lane_f = lane.astype(F32)
    out_lane = lax.broadcasted_iota(jnp.int32, idx_ref.shape, 1)
    out = jnp.zeros(idx_ref.shape, F32)
    for i in range(k):
        m = jnp.max(x, axis=-1, keepdims=True)
        pick = jnp.min(jnp.where(x == m, lane_f, 1e9), axis=-1, keepdims=True)
        out = jnp.where(out_lane == i, pick, out)
        x = jnp.where(lane_f == pick, -3.0e38, x)
    idx_ref[...] = out.astype(jnp.int32)


def _nsa_sample_topk(imp, cur):
    nb, w = imp.shape
    return pl.pallas_call(
        functools.partial(_topk_kernel, cur=cur, k=min(NSA_TOPK, cur + 1)),
        out_shape=jax.ShapeDtypeStruct((nb, LANES), jnp.int32),
        in_specs=[pl.BlockSpec((nb, w), lambda: (0, 0))],
        out_specs=pl.BlockSpec((nb, LANES), lambda: (0, 0)),
        name="nsa_sample_topk",
    )(imp)


def _nsa_s2_kernel(pt_ref, idx_ref, cache_ref, qr_ref, seln_ref, winn_ref, winb_ref, g_ref, ocmp_ref, o_ref,
                   buf, sem, *, base, past):
    b = pl.program_id(0)
    n_past = past // NSA_SEL
    per_page = PAGE_SIZE // NSA_SEL
    nk = NSA_TOPK * NSA_SEL

    def issue(step, slot, start):
        for k in range(NSA_TOPK):
            n = jnp.minimum(idx_ref[step, k], n_past - 1)
            page = pt_ref[step, n // per_page] + base
            row = pl.multiple_of((n % per_page) * NSA_SEL, NSA_SEL)
            cp = pltpu.make_async_copy(cache_ref.at[page, pl.ds(row, NSA_SEL)],
                                       buf.at[slot, pl.ds(k * NSA_SEL, NSA_SEL)], sem.at[slot])
            if start:
                cp.start()
            else:
                cp.wait()

    slot = _prefetch_schedule(b, pl.num_programs(0), issue)

    qr = qr_ref[0]
    lane = lax.broadcasted_iota(jnp.int32, (1, nk), 1)
    tok = jnp.zeros((1, nk), jnp.int32)
    is_past = jnp.zeros((1, nk), jnp.int32)
    tail_ok = jnp.int32(0)
    for k in range(NSA_TOPK):
        ik = idx_ref[b, k]
        in_k = lane // NSA_SEL == k
        tok = jnp.where(in_k, ik * NSA_SEL + lane % NSA_SEL, tok)
        is_past = jnp.where(in_k, (ik < n_past).astype(jnp.int32), is_past)
        tail_ok = tail_ok | ((ik >= n_past) & (ik * NSA_SEL <= past)).astype(jnp.int32)
    kv = buf[slot].astype(BF16)
    s_sel = jnp.where((is_past > 0) & (tok <= past), _dot_nt(qr, kv), NEG)
    s_new = jnp.where(tail_ok > 0, _rowdot(qr, seln_ref[0]), NEG)
    (e_sel, e_new), l = _softmax_parts([s_sel, s_new])
    o_sel = (_dot(e_sel.astype(BF16), kv) + e_new * seln_ref[0]) / l

    wb = winb_ref.shape[1]
    kvw = winb_ref[0].astype(BF16)
    dist = wb - lax.broadcasted_iota(jnp.int32, (1, wb), 1)
    s_win = jnp.where((dist >= 0) & (dist <= NSA_WINDOW) & (past - dist >= 0), _dot_nt(qr, kvw), NEG)
    s_wn = _rowdot(qr, winn_ref[0])
    (e_win, e_wn), lw = _softmax_parts([s_win, s_wn])
    o_win = (_dot(e_win.astype(BF16), kvw) + e_wn * winn_ref[0]) / lw

    g = g_ref[0]
    o = g[:, 0:1] * ocmp_ref[0] + g[:, 1:2] * o_sel[:, HEAD_DIM:] + g[:, 2:3] * o_win[:, HEAD_DIM:]
    o_ref[0] = o.astype(o_ref.dtype)


def _nsa_sample_attend(page_table, idx, cache, qr8, sel_new, win_new, win_buf, gates8, o_cmp, base, win_base, past):
    nb = qr8.shape[0]
    wb = win_buf.shape[1]

    def per_b(r, w):
        return pl.BlockSpec((1, r, w), lambda b, pt, ix: (b, 0, 0))

    grid_spec = pltpu.PrefetchScalarGridSpec(
        num_scalar_prefetch=2,
        grid=(nb,),
        in_specs=[
            pl.BlockSpec(memory_space=pl.ANY),
            per_b(N_HEADS, LANES), per_b(1, LANES), per_b(1, LANES),
            pl.BlockSpec((1, wb, LANES), lambda b, pt, ix: (win_base + b, 0, 0)),
            per_b(N_HEADS, LANES), per_b(N_HEADS, HEAD_DIM),
        ],
        out_specs=per_b(N_HEADS, HEAD_DIM),
        scratch_shapes=[pltpu.VMEM((2, NSA_TOPK * NSA_SEL, LANES), F32), pltpu.SemaphoreType.DMA((2,))],
    )
    return pl.pallas_call(
        functools.partial(_nsa_s2_kernel, base=base, past=past),
        out_shape=jax.ShapeDtypeStruct((nb, N_HEADS, HEAD_DIM), BF16),
        grid_spec=grid_spec,
        compiler_params=_cparams(("arbitrary",)),
        name="nsa_sample_attend",
    )(page_table, idx, cache, qr8, sel_new, win_new, win_buf, gates8, o_cmp)


def _mla_s_kernel(pt_ref, cache_ref, q_ref, new_ref, wuv_ref, o_ref, buf, sem, m_scr, l_scr, acc_scr,
                  *, base, past, pch):
    b = pl.program_id(0)
    c = pl.program_id(1)
    nch = pl.num_programs(1)
    step = b * nch + c
    rows = pch * PAGE_SIZE
    width = cache_ref.shape[2]

    def issue(st, slot, start):
        sb = st // nch
        sc = st % nch

        def body(j, carry):
            page = pt_ref[sb, sc * pch + j] + base
            dst = buf.at[slot, pl.ds(pl.multiple_of(j * PAGE_SIZE, PAGE_SIZE), PAGE_SIZE)]
            cp = pltpu.make_async_copy(cache_ref.at[page], dst, sem.at[slot])
            if start:
                cp.start()
            else:
                cp.wait()
            return carry

        lax.fori_loop(0, pch, body, 0)

    slot = _prefetch_schedule(step, pl.num_programs(0) * nch, issue)

    @pl.when(c == 0)
    def _():
        m_scr[...] = jnp.full_like(m_scr, NEG)
        l_scr[...] = jnp.zeros_like(l_scr)
        acc_scr[...] = jnp.zeros_like(acc_scr)

    q = q_ref[0][:, :width]
    lat = buf[slot].astype(BF16)
    k_pos = c * rows + lax.broadcasted_iota(jnp.int32, (1, rows), 1)
    s = jnp.where(k_pos <= past, _dot_nt(q, lat), NEG)
    m, l, acc = _online_update(s, lat, m_scr[...], l_scr[...], acc_scr[...])
    m_scr[...] = m
    l_scr[...] = l
    acc_scr[...] = acc

    @pl.when(c == nch - 1)
    def _():
        new = new_ref[0][:, :width]
        s_new = _rowdot(q, new)
        m2 = jnp.maximum(m, s_new)
        alpha = jnp.exp(m - m2)
        e_new = jnp.exp(s_new - m2)
        o_lat = ((alpha * acc + e_new * new) / (alpha * l + e_new))[:, :MLA_DC].astype(BF16)
        full = _dot(o_lat, wuv_ref[...])
        head = lax.broadcasted_iota(jnp.int32, (N_HEADS, HEAD_DIM), 0)
        o = jnp.zeros((N_HEADS, HEAD_DIM), F32)
        for h in range(N_HEADS):
            o = jnp.where(head == h, full[:, h * HEAD_DIM:(h + 1) * HEAD_DIM], o)
        o_ref[0] = o.astype(o_ref.dtype)


def _mla_sample(page_table, cache, qc8, lat_new, wuv, base, past, pch):
    nb = qc8.shape[0]
    n_pages = past // PAGE_SIZE
    grid_spec = pltpu.PrefetchScalarGridSpec(
        num_scalar_prefetch=1,
        grid=(nb, n_pages // pch),
        in_specs=[
            pl.BlockSpec(memory_space=pl.ANY),
            pl.BlockSpec((1, N_HEADS, MLA_QW), lambda b, c, pt: (b, 0, 0)),
            pl.BlockSpec((1, 1, MLA_QW), lambda b, c, pt: (b, 0, 0)),
            pl.BlockSpec(wuv.shape, lambda b, c, pt: (0, 0)),
        ],
        out_specs=pl.BlockSpec((1, N_HEADS, HEAD_DIM), lambda b, c, pt: (b, 0, 0)),
        scratch_shapes=[pltpu.VMEM((2, pch * PAGE_SIZE, cache.shape[2]), F32), pltpu.SemaphoreType.DMA((2,)),
                        pltpu.VMEM((N_HEADS, 1), F32), pltpu.VMEM((N_HEADS, 1), F32),
                        pltpu.VMEM((N_HEADS, cache.shape[2]), F32)],
    )
    return pl.pallas_call(
        functools.partial(_mla_s_kernel, base=base, past=past, pch=pch),
        out_shape=jax.ShapeDtypeStruct((nb, N_HEADS, HEAD_DIM), BF16),
        grid_spec=grid_spec,
        compiler_params=_cparams(("arbitrary", "arbitrary")),
        name="mla_sample",
    )(page_table, cache, qc8, lat_new, wuv)


def _dot_split(x, w, parts):
    pieces = _split3(x)[:parts]
    out = _dot(pieces[0], w)
    for p in pieces[1:]:
        out = out + _dot(p, w)
    return out


def _fox_bias_kernel(pt_ref, cache_ref, u_ref, v_ref, e_ref, o_ref, buf, sem, *, base, past, group):
    i = pl.program_id(0)
    n_pages = past // PAGE_SIZE

    def issue(step, slot, start):
        for g in range(group):
            _page_dma(cache_ref, pt_ref, step * group + g, base,
                      lambda j, g=g: buf.at[slot, pl.ds(g * n_pages + j, 1)],
                      sem.at[slot], n_pages, start)

    slot = _prefetch_schedule(i, pl.num_programs(0), issue)
    x = buf[slot]
    within = _dot_split(x, u_ref[...], 3)
    tot = _dot_split(x, v_ref[...], 3)
    upper = (lax.broadcasted_iota(jnp.int32, (n_pages, n_pages), 1)
             > lax.broadcasted_iota(jnp.int32, (n_pages, n_pages), 0)).astype(BF16)
    later = jnp.concatenate(
        [_dot_split_lhs(upper, tot[g * n_pages:(g + 1) * n_pages]) for g in range(group)], axis=0)
    o_ref[...] = within + _dot_split(later, e_ref[...], 3)


def _dot_split_lhs(w, x):
    hi, mid, lo = _split3(x)
    return (_dot(w, hi) + _dot(w, mid)) + _dot(w, lo)


def _fox_sample_bias(page_table, cache, u, v, e, base, past, group):
    nb = page_table.shape[0]
    n_pages = past // PAGE_SIZE
    pw = cache.shape[2]
    grid_spec = pltpu.PrefetchScalarGridSpec(
        num_scalar_prefetch=1,
        grid=(nb // group,),
        in_specs=[
            pl.BlockSpec(memory_space=pl.ANY),
            pl.BlockSpec(u.shape, lambda i, pt: (0, 0)),
            pl.BlockSpec(v.shape, lambda i, pt: (0, 0)),
            pl.BlockSpec(e.shape, lambda i, pt: (0, 0)),
        ],
        out_specs=pl.BlockSpec((group * n_pages, pw), lambda i, pt: (i, 0)),
        scratch_shapes=[pltpu.VMEM((2, group * n_pages, pw), F32), pltpu.SemaphoreType.DMA((2,))],
    )
    return pl.pallas_call(
        functools.partial(_fox_bias_kernel, base=base, past=past, group=group),
        out_shape=jax.ShapeDtypeStruct((nb * n_pages, pw), F32),
        grid_spec=grid_spec,
        compiler_params=_cparams(("arbitrary",)),
        name="fox_sample_bias",
    )(page_table, cache, u, v, e)


def _fox_s_kernel(pt_ref, cache_ref, q_ref, bias_ref, new_ref, lnew_ref, o_ref, buf, sem, *, base, past):
    b = pl.program_id(0)
    n_pages = past // PAGE_SIZE

    def issue(step, slot, start):
        _page_dma(cache_ref, pt_ref, step, base,
                  lambda j: buf.at[slot, pl.ds(pl.multiple_of(j * PAGE_SIZE, PAGE_SIZE), PAGE_SIZE)],
                  sem.at[slot], n_pages, start)

    slot = _prefetch_schedule(b, pl.num_programs(0), issue)
    q = (q_ref[0] * ATT_SCALE).astype(BF16)
    kv = buf[slot].astype(BF16)
    bias =jnp.concatenate([bias_ref[j] for j in range(n_pages)], axis=1)
    k_pos = lax.broadcasted_iota(jnp.int32, (1, past), 1)
    s = jnp.where(k_pos <= past, _dot_nt(q, kv) + bias + lnew_ref[0][:, 0:1], NEG)
    s_new = _rowdot(q, new_ref[0])
    (e, e_new), l = _softmax_parts([s, s_new])
    o = (_dot(e.astype(BF16), kv) + e_new * new_ref[0]) / l
    o_ref[0] = o[:, HEAD_DIM:].astype(o_ref.dtype)


def _fox_sample(page_table, cache, q8, bias, kv_new, lnew, base, past):
    nb = q8.shape[0]
    n_pages = past // PAGE_SIZE

    def per_b(r, w):
        return pl.BlockSpec((1, r, w), lambda b, pt: (b, 0, 0))

    grid_spec = pltpu.PrefetchScalarGridSpec(
        num_scalar_prefetch=1,
        grid=(nb,),
        in_specs=[
            pl.BlockSpec(memory_space=pl.ANY),
            per_b(N_HEADS, LANES),
            pl.BlockSpec((n_pages, N_HEADS, PAGE_SIZE), lambda b, pt: (b, 0, 0)),
            per_b(1, LANES), per_b(N_HEADS, LANES),
        ],
        out_specs=per_b(N_HEADS, HEAD_DIM),
        scratch_shapes=[pltpu.VMEM((2, past, LANES), F32), pltpu.SemaphoreType.DMA((2,))],
    )
    return pl.pallas_call(
        functools.partial(_fox_s_kernel, base=base, past=past),
        out_shape=jax.ShapeDtypeStruct((nb, N_HEADS, HEAD_DIM), BF16),
        grid_spec=grid_spec,
        compiler_params=_cparams(("arbitrary",)),
        name="fox_sample",
    )(page_table, cache, q8, bias, kv_new, lnew)


def _gmlp_s_kernel(u_ref, v_ref, w_ref, b_ref, o_ref):
    o_ref[...] = (u_ref[...] * (v_ref[...] * w_ref[...] + b_ref[...])).astype(o_ref.dtype)


def _gmlp_sample(u, v, w00, b0, row0, nb):
    blk = row0 // nb
    return pl.pallas_call(
        _gmlp_s_kernel,
        out_shape=jax.ShapeDtypeStruct((nb, MIX_W), BF16),
        grid=(1,),
        in_specs=[pl.BlockSpec((nb, MIX_W), lambda i: (blk, 0)), pl.BlockSpec((nb, MIX_W), lambda i: (blk, 0)),
                  pl.BlockSpec((1, MIX_W), lambda i: (0, 0)), pl.BlockSpec((1, MIX_W), lambda i: (0, 0))],
        out_specs=pl.BlockSpec((nb, MIX_W), lambda i: (0, 0)),
        name="gmlp_sample",
    )(u, v, w00, b0)


DMA_ISSUE_UNROLL = 8


def _wait_whole(buf, slot, sem):
    pltpu.make_async_copy(buf.at[slot], buf.at[slot], sem).wait()


def _lane_page_dma(cache_ref, pt_ref, b, base, buf, slot, sem, n_pages, start, page0=0):
    if not start:
        _wait_whole(buf, slot, sem)
        return

    def body(j, c):
        dst = buf.at[slot, :, pl.ds(pl.multiple_of(j * PAGE_SIZE, PAGE_SIZE), PAGE_SIZE)]
        pltpu.make_async_copy(cache_ref.at[pt_ref[b, page0 + j] + base], dst, sem).start()
        return c

    lax.fori_loop(0, n_pages, body, 0, unroll=DMA_ISSUE_UNROLL)


def _split2(x):
    hi = x.astype(BF16)
    return hi, (x - hi.astype(F32)).astype(BF16)


def _nsa_t1_kernel(pt_ref, cache_ref, q_ref, new_ref, w_ref, wt_ref, pool_ref, imp_ref, o_ref, buf, sem,
                   *, base, past):
    b = pl.program_id(0)
    n_pages = past // PAGE_SIZE
    gt, nbg = pool_ref.shape
    hb = nbg // 2

    def issue(step, slot, start):
        _lane_page_dma(cache_ref, pt_ref, step, base, buf, slot, sem.at[slot], n_pages, start)

    slot = _prefetch_schedule(b, pl.num_programs(0), issue)

    wt = jnp.concatenate([wt_ref[...]] * (gt // LANES), axis=1)
    pool = pool_ref[...]
    groups = []
    for g in range(past // gt):
        hi, mid = _split2(buf[slot, :, g * gt:(g + 1) * gt] * wt)
        groups.append((_dot(hi, pool) + _dot(mid, pool)) * (1.0 / NSA_CMP))
    cmp_t = jnp.concatenate(groups, axis=1).astype(BF16)
    te = new_ref[0] * w_ref[0:1, :] * (1.0 / NSA_CMP)
    to = jnp.zeros_like(te)

    q = q_ref[0]
    lane = lax.broadcasted_iota(jnp.int32, (1, past // NSA_CMP), 1)
    blk = (lane // nbg) * nbg + 2 * (lane % hb) + (lane % nbg) // hb
    ok = blk * NSA_CMP + NSA_CMP - 1 <= past
    ns = past // NSA_SEL
    ok_te = ns * NSA_SEL + NSA_CMP - 1 <= past
    ok_to = ns * NSA_SEL + NSA_SEL - 1 <= past
    s = jnp.where(ok, _dot(q, cmp_t), NEG)
    s_te = jnp.where(ok_te, _rowdot(q, te), NEG)
    s_to = jnp.where(ok_to, _rowdot(q, to), NEG)
    (e, e_te, e_to), l = _softmax_parts([s, s_te, s_to])
    p = e / l * jnp.where(ok, 1.0, 0.0)
    p_te = e_te / l * (1.0 if ok_te else 0.0)
    p_to = e_to / l * (1.0 if ok_to else 0.0)
    o = _dot_nt(p.astype(BF16), cmp_t) + p_te * te + p_to * to
    o_ref[0] = o[:, HEAD_DIM:]
    ph = jnp.sum(p, axis=0, keepdims=True)
    imp = [ph[:, g * nbg:g * nbg + hb] + ph[:, g * nbg + hb:(g + 1) * nbg] for g in range(past // gt)]
    imp_tail = jnp.sum(p_te + p_to, axis=0, keepdims=True)
    imp_ref[0] = jnp.concatenate(imp + [jnp.broadcast_to(imp_tail, (1, LANES))], axis=1)


def _nsa_sample_cmp_t(page_table, cache, q8, kv_new, w, wt, pool, base, past):
    nb = q8.shape[0]
    ns = past // NSA_SEL
    grid_spec = pltpu.PrefetchScalarGridSpec(
        num_scalar_prefetch=1,
        grid=(nb,),
        in_specs=[
            pl.BlockSpec(memory_space=pl.ANY),
            pl.BlockSpec((1, N_HEADS, LANES), lambda b, pt: (b, 0, 0)),
            pl.BlockSpec((1, 1, LANES), lambda b, pt: (b, 0, 0)),
            pl.BlockSpec(w.shape, lambda b, pt: (0, 0)),
            pl.BlockSpec(wt.shape, lambda b, pt: (0, 0)),
            pl.BlockSpec(pool.shape, lambda b, pt: (0, 0)),
        ],
        out_specs=(pl.BlockSpec((1, 1, ns + LANES), lambda b, pt: (b, 0, 0)),
                   pl.BlockSpec((1, N_HEADS, HEAD_DIM), lambda b, pt: (b, 0, 0))),
        scratch_shapes=[pltpu.VMEM((2, LANES, past), F32), pltpu.SemaphoreType.DMA((2,))],
    )
    return pl.pallas_call(
        functools.partial(_nsa_t1_kernel, base=base, past=past),
        out_shape=(jax.ShapeDtypeStruct((nb, 1, ns + LANES), F32),
                   jax.ShapeDtypeStruct((nb, N_HEADS, HEAD_DIM), F32)),
        grid_spec=grid_spec,
        compiler_params=_cparams(("arbitrary",)),
        name="nsa_sample_cmp",
    )(page_table, cache, q8, kv_new, w, wt, pool)


def _nsa_t2_kernel(pt_ref, idx_ref, cache_ref, qr_ref, seln_ref, winn_ref, winb_ref, g_ref, ocmp_ref, o_ref,
                   buf, sem, *, base, past):
    b = pl.program_id(0)
    n_past = past // NSA_SEL
    per_page = PAGE_SIZE // NSA_SEL
    nk = NSA_TOPK * PAGE_SIZE

    def issue(step, slot, start):
        for k in range(NSA_TOPK):
            n = jnp.minimum(idx_ref[step, k], n_past - 1)
            page = pt_ref[step, n // per_page] + base
            cp = pltpu.make_async_copy(cache_ref.at[page], buf.at[slot, :, pl.ds(k * PAGE_SIZE, PAGE_SIZE)],
                                       sem.at[slot])
            if start:
                cp.start()
            else:
                cp.wait()

    slot = _prefetch_schedule(b, pl.num_programs(0), issue)

    qr = qr_ref[0]
    lane = lax.broadcasted_iota(jnp.int32, (1, nk), 1)
    in_page = lane % PAGE_SIZE
    tok = jnp.zeros((1, nk), jnp.int32)
    valid = jnp.zeros((1, nk), jnp.int32)
    tail_ok = jnp.int32(0)
    for k in range(NSA_TOPK):
        ik = idx_ref[b, k]
        n = jnp.minimum(ik, n_past - 1)
        in_k = lane // PAGE_SIZE == k
        tok = jnp.where(in_k, (n // per_page) * PAGE_SIZE + in_page, tok)
        mine = jnp.where(in_page // NSA_SEL == n % per_page, (ik < n_past).astype(jnp.int32), 0)
        valid = jnp.where(in_k, mine, valid)
        tail_ok = tail_ok | ((ik >= n_past) & (ik * NSA_SEL <= past)).astype(jnp.int32)
    kv = buf[slot].astype(BF16)
    s_sel = jnp.where(jnp.where(tok <= past, valid, 0) > 0, _dot(qr, kv), NEG)
    s_new = jnp.where(tail_ok > 0, _rowdot(qr, seln_ref[0]), NEG)
    (e_sel, e_new), l = _softmax_parts([s_sel, s_new])
    o_sel = (_dot_nt(e_sel.astype(BF16), kv) + e_new * seln_ref[0]) / l

    wb = winb_ref.shape[2]
    kvw = winb_ref[0].astype(BF16)
    dist = wb - lax.broadcasted_iota(jnp.int32, (1, wb), 1)
    in_win = jnp.where(dist <= NSA_WINDOW, jnp.where(past - dist >= 0, 1, 0), 0)
    s_win = jnp.where(in_win > 0, _dot(qr, kvw), NEG)
    s_wn = _rowdot(qr, winn_ref[0])
    (e_win, e_wn), lw = _softmax_parts([s_win, s_wn])
    o_win = (_dot_nt(e_win.astype(BF16), kvw) + e_wn * winn_ref[0]) / lw

    g = g_ref[0]
    o = g[:, 0:1] * ocmp_ref[0] + g[:, 1:2] * o_sel[:, HEAD_DIM:] + g[:, 2:3] * o_win[:, HEAD_DIM:]
    o_ref[0] = o.astype(o_ref.dtype)


def _nsa_sample_attend_t(page_table, idx, cache, qr8, sel_new, win_new, win_t, gates8, o_cmp, base, win_base, past):
    nb = qr8.shape[0]
    wb = win_t.shape[2]

    def per_b(r, w):
        return pl.BlockSpec((1, r, w), lambda b, pt, ix: (b, 0, 0))

    grid_spec = pltpu.PrefetchScalarGridSpec(
        num_scalar_prefetch=2,
        grid=(nb,),
        in_specs=[
            pl.BlockSpec(memory_space=pl.ANY),
            per_b(N_HEADS, LANES), per_b(1, LANES), per_b(1, LANES),
            pl.BlockSpec((1, LANES, wb), lambda b, pt, ix: (win_base + b, 0, 0)),
            per_b(N_HEADS, LANES), per_b(N_HEADS, HEAD_DIM),
        ],
        out_specs=per_b(N_HEADS, HEAD_DIM),
        scratch_shapes=[pltpu.VMEM((2, LANES, NSA_TOPK * PAGE_SIZE), F32), pltpu.SemaphoreType.DMA((2,))],
    )
    return pl.pallas_call(
        functools.partial(_nsa_t2_kernel, base=base, past=past),
        out_shape=jax.ShapeDtypeStruct((nb, N_HEADS, HEAD_DIM), BF16),
        grid_spec=grid_spec,
        compiler_params=_cparams(("arbitrary",)),
        name="nsa_sample_attend",
    )(page_table, idx, cache, qr8, sel_new, win_new, win_t, gates8, o_cmp)


def _mla_t_kernel(pt_ref, cache_ref, q_ref, new_ref, wuv_ref, o_ref, buf, sem, m_scr, l_scr, acc_scr,
                  *, base, past, pch):
    b = pl.program_id(0)
    c = pl.program_id(1)
    nch = pl.num_programs(1)
    step = b * nch + c
    cols = pch * PAGE_SIZE
    width = cache_ref.shape[1]

    def issue(st, slot, start):
        _lane_page_dma(cache_ref, pt_ref, st // nch, base, buf, slot, sem.at[slot], pch, start,
                       page0=(st % nch) * pch)

    slot = _prefetch_schedule(step, pl.num_programs(0) * nch, issue)

    @pl.when(c == 0)
    def _():
        m_scr[...] = jnp.full_like(m_scr, NEG)
        l_scr[...] = jnp.zeros_like(l_scr)
        acc_scr[...] = jnp.zeros_like(acc_scr)

    q = q_ref[0]
    lat_t = buf[slot].astype(BF16)
    k_pos = c * cols + lax.broadcasted_iota(jnp.int32, (1, cols), 1)
    s = jnp.where(k_pos <= past, _dot(q[:, :width], lat_t), NEG)
    m_old = m_scr[...]
    m = jnp.maximum(m_old, jnp.max(s, axis=-1, keepdims=True))
    alpha = jnp.exp(m_old - m)
    p = jnp.exp(s - m)
    l = alpha * l_scr[...] + jnp.sum(p, axis=-1, keepdims=True)
    acc = alpha * acc_scr[...] + _dot_nt(p.astype(BF16), lat_t[:MLA_DC])
    m_scr[...] = m
    l_scr[...] = l
    acc_scr[...] = acc

    @pl.when(c == nch - 1)
    def _():
        new = new_ref[0]
        s_new = _rowdot(q, new)
        m2 = jnp.maximum(m, s_new)
        a2 = jnp.exp(m - m2)
        e_new = jnp.exp(s_new - m2)
        o_lat = ((a2 * acc + e_new * new[:, :MLA_DC]) / (a2 * l + e_new)).astype(BF16)
        full = _dot(o_lat, wuv_ref[...])
        head = lax.broadcasted_iota(jnp.int32, (N_HEADS, HEAD_DIM), 0)
        o = jnp.zeros((N_HEADS, HEAD_DIM), F32)
        for h in range(N_HEADS):
            o = jnp.where(head == h, full[:, h * HEAD_DIM:(h + 1) * HEAD_DIM], o)
        o_ref[0] = o.astype(o_ref.dtype)


def _mla_sample_t(page_table, cache, qc8, lat_new, wuv, base, past, pch):
    nb = qc8.shape[0]
    n_pages = past // PAGE_SIZE
    width = cache.shape[1]
    grid_spec = pltpu.PrefetchScalarGridSpec(
        num_scalar_prefetch=1,
        grid=(nb, n_pages // pch),
        in_specs=[
            pl.BlockSpec(memory_space=pl.ANY),
            pl.BlockSpec((1, N_HEADS, MLA_QW), lambda b, c, pt: (b, 0, 0)),
            pl.BlockSpec((1, 1, MLA_QW), lambda b, c, pt: (b, 0, 0)),
            pl.BlockSpec(wuv.shape, lambda b, c, pt: (0, 0)),
        ],
        out_specs=pl.BlockSpec((1, N_HEADS, HEAD_DIM), lambda b, c, pt: (b, 0, 0)),
        scratch_shapes=[pltpu.VMEM((2, width, pch * PAGE_SIZE), F32), pltpu.SemaphoreType.DMA((2,)),
                        pltpu.VMEM((N_HEADS, 1), F32), pltpu.VMEM((N_HEADS, 1), F32),
                        pltpu.VMEM((N_HEADS, MLA_DC), F32)],
    )
    return pl.pallas_call(
        functools.partial(_mla_t_kernel, base=base, past=past, pch=pch),
        out_shape=jax.ShapeDtypeStruct((nb, N_HEADS, HEAD_DIM), BF16),
        grid_spec=grid_spec,
        compiler_params=_cparams(("arbitrary", "arbitrary")),
        name="mla_sample",
    )(page_table, cache, qc8, lat_new, wuv)


def _fox_t_kernel(pt_ref, cache_ref, lcache_ref, q_ref, new_ref, lnew_ref, u_ref, o_ref,
                  buf, lbuf, sem, lsem, *, base, past):
    b = pl.program_id(0)
    n_pages = past // PAGE_SIZE

    def issue(step, slot, start):
        _lane_page_dma(cache_ref, pt_ref, step, base, buf, slot, sem.at[slot], n_pages, start)
        if not start:
            _wait_whole(lbuf, slot, lsem.at[slot])
            return

        def body(j, c):
            dst = lbuf.at[slot, pl.ds(pl.multiple_of(j * N_HEADS, N_HEADS), N_HEADS)]
            pltpu.make_async_copy(lcache_ref.at[pt_ref[step, j] + base], dst, lsem.at[slot]).start()
            return c

        lax.fori_loop(0, n_pages, body, 0, unroll=DMA_ISSUE_UNROLL)

    slot = _prefetch_schedule(b, pl.num_programs(0), issue)

    x = lbuf[slot]
    within = _dot_split(x, u_ref[...], 3)
    totals = jnp.sum(x, axis=-1, keepdims=True)
    later = jnp.zeros((N_HEADS, 1), F32)
    pieces = [None] * n_pages
    for j in reversed(range(n_pages)):
        pieces[j] = within[j * N_HEADS:(j + 1) * N_HEADS] + later
        later = later + totals[j * N_HEADS:(j + 1) * N_HEADS]
    bias = jnp.concatenate(pieces, axis=1)

    q = (q_ref[0] * ATT_SCALE).astype(BF16)
    kv = buf[slot].astype(BF16)
    k_pos = lax.broadcasted_iota(jnp.int32, (1, past), 1)
    s = jnp.where(k_pos <= past, _dot(q, kv) + bias + lnew_ref[0][:, 0:1], NEG)
    s_new = _rowdot(q, new_ref[0])
    (e, e_new), l = _softmax_parts([s, s_new])
    o = (_dot_nt(e.astype(BF16), kv) + e_new * new_ref[0]) / l
    o_ref[0] = o[:, HEAD_DIM:].astype(o_ref.dtype)


def _fox_sample_t(page_table, cache, lcache, q8, kv_new, lnew, u, base, past):
    nb = q8.shape[0]
    n_pages = past // PAGE_SIZE

    def per_b(r, w):
        return pl.BlockSpec((1, r, w), lambda b, pt: (b, 0, 0))

    def const(a):
        return pl.BlockSpec(a.shape, lambda b, pt: (0, 0))

    grid_spec = pltpu.PrefetchScalarGridSpec(
        num_scalar_prefetch=1,
        grid=(nb,),
        in_specs=[
            pl.BlockSpec(memory_space=pl.ANY), pl.BlockSpec(memory_space=pl.ANY),
            per_b(N_HEADS, LANES), per_b(1, LANES), per_b(N_HEADS, LANES),
            const(u),
        ],
        out_specs=per_b(N_HEADS, HEAD_DIM),
        scratch_shapes=[pltpu.VMEM((2, LANES, past), F32), pltpu.VMEM((2, n_pages * N_HEADS, PAGE_SIZE), F32),
                        pltpu.SemaphoreType.DMA((2,)), pltpu.SemaphoreType.DMA((2,))],
    )
    return pl.pallas_call(
        functools.partial(_fox_t_kernel, base=base, past=past),
        out_shape=jax.ShapeDtypeStruct((nb, N_HEADS, HEAD_DIM), BF16),
        grid_spec=grid_spec,
        compiler_params=_cparams(("arbitrary",)),
        name="fox_sample",
    )(page_table, cache, lcache, q8, kv_new, lnew, u)


def _kv_pages_t(cache):
    l, p, t = cache.shape[:3]
    return jnp.transpose(cache, (0, 1, 3, 4, 5, 2)).reshape(l * p, -1, t)


def _pages_t(cache):
    l, p, t, c = cache.shape
    return jnp.transpose(cache, (0, 1, 3, 2)).reshape(l * p, c, t)


def _sample_consts(past):
    n_pages = past // PAGE_SIZE
    gt = min(past, 32 * PAGE_SIZE)
    nbg = gt // NSA_CMP
    n = np.arange(gt) // NSA_CMP
    col = (n % 2) * (nbg // 2) + n // 2
    pool = col[:, None] == np.arange(nbg)[None, :]
    t = np.arange(PAGE_SIZE)
    u = t[:, None] > t[None, :]
    return tuple(jnp.asarray(a, dtype=BF16) for a in (pool, u))


def _swap_rot(w, rot):
    half = rot // 2
    return jnp.concatenate([-w[..., half:rot], w[..., :half], jnp.zeros_like(w[..., rot:])], axis=-1)


def _pad_last(w, n):
    return jnp.pad(w, [(0, 0)] * (w.ndim - 1) + [(0, n - w.shape[-1])])


def _small_proj_weight(w_in):
    d = w_in.shape[0]
    off = [0]

    def take(n):
        s = w_in[:, off[0]:off[0] + n]
        off[0] += n
        return s

    nsa_q = take(MIX_W)
    nsa_kv = take(6 * HEAD_DIM)
    nsa_g = take(3 * N_HEADS)
    gm_u, gm_v, mla_cq = take(MIX_W), take(MIX_W), take(MLA_DQ)
    mla_ckv, mla_kr = take(MLA_DC), take(MLA_DR)
    fox_q, fox_kv, fox_f = take(MIX_W), take(2 * HEAD_DIM), take(N_HEADS)
    nsa_qs = _swap_rot(nsa_q.reshape(d, N_HEADS, HEAD_DIM), ROT_DIM).reshape(d, MIX_W)
    sel_ks = _swap_rot(nsa_kv[:, 2 * HEAD_DIM:3 * HEAD_DIM], ROT_DIM)
    win_ks = _swap_rot(nsa_kv[:, 4 * HEAD_DIM:5 * HEAD_DIM], ROT_DIM)
    segs = [nsa_q, nsa_qs, gm_u, gm_v, mla_cq, fox_q,
            nsa_kv[:, :LANES], nsa_kv[:, LANES:2 * LANES], nsa_kv[:, 2 * LANES:],
            _pad_last(sel_ks, LANES), _pad_last(win_ks, LANES), _pad_last(nsa_g, LANES),
            mla_ckv, _pad_last(mla_kr, LANES), _pad_last(_swap_rot(mla_kr, MLA_DR), LANES),
            fox_kv, _pad_last(fox_f, LANES)]
    w_small = jnp.concatenate(segs, axis=1).astype(BF16)
    assert w_small.shape[1] == Z_W
    return w_small, w_in[:, off[0]:].astype(BF16)


def _block_diag(w):
    h, r, c = w.shape
    eye = jnp.eye(h, dtype=bool)
    return jnp.where(eye[:, None, :, None], w[:, :, None, :], 0).reshape(h * r, h * c)


def _mla_weights(wq, wuk, wuv):
    dq = wq.shape[0]
    wq = wq.reshape(dq, N_HEADS, HEAD_DIM + MLA_DR)
    rope = wq[:, :, HEAD_DIM:]
    lead = jnp.zeros((dq, N_HEADS, MLA_DC), wq.dtype)
    wa = _pad_last(jnp.concatenate([lead, rope], axis=-1), MLA_QW).reshape(dq, N_HEADS * MLA_QW)
    wb = _pad_last(jnp.concatenate([lead, _swap_rot(rope, MLA_DR)], axis=-1), MLA_QW).reshape(dq, N_HEADS * MLA_QW)
    wqn = wq[:, :, :HEAD_DIM].reshape(dq, MIX_W)
    wbd = _block_diag(_pad_last(jnp.transpose(wuk, (1, 2, 0)), MLA_QW))
    wuv_bd = _block_diag(jnp.transpose(wuv, (1, 0, 2)))
    return (wqn.astype(BF16), wbd.astype(BF16), wa.astype(BF16), wb.astype(BF16), wuv_bd.astype(BF16),
            wuv.reshape(MLA_DC, MIX_W).astype(BF16))


def _rope_tables(pos):
    def cs(half, theta):
        inv = theta ** (-jnp.arange(half, dtype=F32) / half)
        ang = pos.astype(F32)[:, None] * inv[None, :]
        return jnp.cos(ang), jnp.sin(ang)

    n = pos.shape[0]
    c, s = cs(ROT_DIM // 2, ROPE_THETA)
    one, zero = jnp.ones((n, HEAD_DIM - ROT_DIM), F32), jnp.zeros((n, HEAD_DIM - ROT_DIM), F32)
    hc = jnp.concatenate([c, c, one], axis=1)
    hs = jnp.concatenate([s, s, zero], axis=1)
    tab_nsa = jnp.concatenate([hc, hc, hs, hs, hc, jnp.ones((n, HEAD_DIM), F32), hs, jnp.zeros((n, HEAD_DIM), F32)],
                              axis=1)
    c, s = cs(MLA_DR // 2, MLA_THETA)
    tail = jnp.zeros((n, MLA_QW - MLA_DC - MLA_DR), F32)
    tab_mla = jnp.concatenate([jnp.ones((n, MLA_DC), F32), c, c, tail, jnp.zeros((n, MLA_DC), F32), s, s, tail], axis=1)
    return tab_nsa, tab_mla


def _fox_bias_consts():
    t1, h1 = np.divmod(np.arange(PAGE_SIZE * N_HEADS), N_HEADS)
    h2, t2 = np.divmod(np.arange(PAGE_SIZE * N_HEADS), PAGE_SIZE)
    u = (h1[:, None] == h2[None, :]) & (t1[:, None] > t2[None, :])
    copies = LANES // N_HEADS
    hv, rv = np.divmod(np.arange(LANES), copies)
    v = h1[:, None] == hv[None, :]
    e = (hv[:, None] == h2[None, :]) & (rv[:, None] == 0)
    return tuple(jnp.asarray(a, dtype=BF16) for a in (u, v, e))


def kernel(x_prompt, x_sample, cache_nsa_cmp, cache_nsa_sel, state_nsa_win, cache_mla, cache_fox_kv, cache_fox_logf, page_table, p_prompt, p_sample, n_ffn1, ffn1_in, ffn1_out, n_mix, w_in, nsa_cmp_w, gm_norm, gm_ws, gm_bs, mla_q_norm, mla_wq, mla_kv_norm, mla_wuk, mla_wuv, fox_bf, w_br, w_o, n_ffn2, ffn2_in, ffn2_out, n_ple, ple_wg, ple_wp, norm_final):
    batch, seq, d = x_prompt.shape
    nb = x_sample.shape[0]
    depth = w_in.shape[0]
    n_pool = cache_mla.shape[1]
    past = page_table.shape[1] * PAGE_SIZE
    wb = state_nsa_win.shape[2]
    n_p = batch * seq
    n = n_p + nb
    tm = 384 if n % 384 == 0 else 128
    tq, tk = 128, 256

    x = jnp.concatenate([x_prompt.reshape(n_p, d), x_sample.reshape(nb, d)], axis=0)
    pos = jnp.concatenate([jnp.arange(n_p, dtype=jnp.int32) % seq, jnp.full((nb,), past, jnp.int32)])
    tab_nsa, tab_mla = _rope_tables(pos)
    pool_m, bias_u = _sample_consts(past)
    c_nsa_cmp = _kv_pages_t(cache_nsa_cmp)
    c_nsa_sel = _kv_pages_t(cache_nsa_sel)
    c_mla = _pages_t(cache_mla)
    c_fox_kv = _kv_pages_t(cache_fox_kv)
    c_fox_logf = _pages_t(cache_fox_logf)
    win_t = jnp.transpose(state_nsa_win, (0, 1, 3, 4, 5, 2))
    win_state = win_t.reshape(depth * nb, LANES, wb)

    st_p, st_s = [], []
    for l in range(depth):
        w_small, w_gate = _small_proj_weight(w_in[l])
        wqn, wbd, wa, wb_, wuv_bd, wuv_flat = _mla_weights(mla_wq[l], mla_wuk[l], mla_wuv[l])
        cmp_w = nsa_cmp_w[l].reshape(NSA_CMP, LANES)
        bs_exp = jnp.repeat(gm_bs[l].T, HEAD_DIM, axis=1)
        w00 = jnp.repeat(gm_ws[l][:, 0, 0], HEAD_DIM).reshape(1, MIX_W)
        b0 = jnp.repeat(gm_bs[l][:, 0], HEAD_DIM).reshape(1, MIX_W)

        x = _ffn(x, n_ffn1[l], ffn1_in[l].astype(BF16), ffn1_out[l].astype(BF16), tm, 1024)
        z = _proj(x, n_mix[l], w_small, tm)
        q, qr, kv_sel, kv_win, gates, gm_u, gm_v, qc, lat, logf = _prep(
            z, tab_nsa, tab_mla, gm_norm[l], mla_q_norm[l], mla_kv_norm[l],
            _pad_last(fox_bf[l].reshape(1, N_HEADS), LANES), wqn, wbd, wa, wb_, tm)
        kv_cmp = z[:, Z_NSA_CMP:Z_NSA_CMP + LANES]
        fox_kv = z[:, Z_FOX_KV:Z_FOX_KV + LANES]

        cmp_e, cmp_o = _nsa_compress(kv_cmp.reshape(n // NSA_SEL, NSA_SEL * LANES), cmp_w,
                                     n_p // NSA_SEL, seq // NSA_SEL)
        o_nsa = _nsa_prompt(q, qr, gates, cmp_e, cmp_o, kv_sel, kv_win, batch, seq, tq, tk)
        o_gm = _gmlp_prompt(gm_u, gm_v, gm_ws[l], bs_exp, n_p, 512)
        o_mla = _mla_prompt(qc, lat, wuv_bd, batch, seq, tq, tk)
        c_col, c_row = _fox_cumsum(logf, batch, seq)
        o_fox = _fox_prompt(z, c_col, c_row, batch, seq, tq, tk)

        base = l * n_pool
        q8 = _pad_last(q[n_p:].reshape(nb, N_HEADS, HEAD_DIM), LANES)
        qr8 = _pad_last(qr[n_p:].reshape(nb, N_HEADS, HEAD_DIM), LANES)
        cmp_wt = jnp.tile(cmp_w.T, (1, LANES // NSA_CMP))
        imp, o_cmp = _nsa_sample_cmp_t(page_table, c_nsa_cmp, q8, kv_cmp[n_p:].reshape(nb, 1, LANES), cmp_w, cmp_wt,
                                       pool_m, base, past)
        idx = _nsa_sample_topk(imp.reshape(nb, -1), past // NSA_SEL)
        gates8 = _pad_last(gates[n_p:, :3 * N_HEADS].reshape(nb, N_HEADS, 3), LANES)
        o_nsa_s = _nsa_sample_attend_t(page_table, idx, c_nsa_sel, qr8, kv_sel[n_p:].reshape(nb, 1, LANES),
                                       kv_win[n_p:].reshape(nb, 1, LANES), win_state, gates8, o_cmp,
                                       base, l * nb, past)
        o_gm_s = _gmlp_sample(gm_u, gm_v, w00, b0, n_p, nb)
        o_mla_s = _mla_sample_t(page_table, c_mla, qc[n_p:].reshape(nb, N_HEADS, MLA_QW),
                                lat[n_p:].reshape(nb, 1, MLA_QW), wuv_flat, base, past, min(64, past // PAGE_SIZE))
        fq8 = _pad_last(z[n_p:, Z_FOX_Q:Z_FOX_Q + MIX_W].reshape(nb, N_HEADS, HEAD_DIM), LANES)
        lnew = jnp.broadcast_to(logf[n_p:, :N_HEADS, None], (nb, N_HEADS, LANES))
        o_fox_s = _fox_sample_t(page_table, c_fox_kv, c_fox_logf, fq8, fox_kv[n_p:].reshape(nb, 1, LANES), lnew,
                                bias_u, base, past)

        outs = jnp.stack([jnp.concatenate([a, b.reshape(nb, MIX_W)], axis=0)
                          for a, b in ((o_nsa, o_nsa_s), (o_gm, o_gm_s), (o_mla, o_mla_s), (o_fox, o_fox_s))])
        x = _merge(x, n_mix[l], outs, w_gate, w_br[l].astype(BF16), w_o[l].astype(BF16), tm)
        x = _ffn(x, n_ffn2[l], ffn2_in[l].astype(BF16), ffn2_out[l].astype(BF16), tm, 1024)
        p_all = jnp.concatenate([p_prompt[l].reshape(n_p, -1), p_sample[l].reshape(nb, -1)], axis=0)
        x = _ple(x, n_ple[l], p_all, ple_wg[l].astype(BF16), ple_wp[l].astype(BF16), norm_final,
                 l == depth - 1, tm)

        def kv5(a, rows):
            return a.reshape(rows, -1, 2, 1, HEAD_DIM)

        win_p = kv5(kv_win[:n_p], batch)
        st_p.append((kv5(kv_cmp[:n_p], batch), kv5(kv_sel[:n_p], batch), win_p[:, seq - min(NSA_WINDOW, seq):],
                     lat[:n_p, :MLA_DC + MLA_DR].reshape(batch, seq, -1), kv5(fox_kv[:n_p], batch),
                     logf[:n_p, :N_HEADS].reshape(batch, seq, N_HEADS)))
        new_t = kv_win[n_p:].reshape(nb, 2, 1, HEAD_DIM, 1)
        win_s = jnp.transpose(jnp.concatenate([win_t[l][..., 1:], new_t], axis=-1), (0, 4, 1, 2, 3))
        st_s.append((kv5(kv_cmp[n_p:], nb), kv5(kv_sel[n_p:], nb), win_s,
                     lat[n_p:, :MLA_DC + MLA_DR].reshape(nb, 1, -1), kv5(fox_kv[n_p:], nb),
                     logf[n_p:, :N_HEADS].reshape(nb, 1, N_HEADS), gm_v[n_p:].reshape(nb, 1, N_HEADS, HEAD_DIM)))

    y = x
    outs_p = [jnp.stack(a) for a in zip(*st_p)]
    outs_s = [jnp.stack(a) for a in zip(*st_s)]
    return (y[:n_p].reshape(batch, seq, d), y[n_p:].reshape(nb, 1, d), *outs_p, *outs_s)
```
